```python
import jax, jax.numpy as jnp
from jax import lax
import numpy as np

D_MODEL = 1024
BATCH = 8
SEQ = 2048
DEPTH = 1

N_HEADS = 8
HEAD_DIM = 64
ATTN_WIDTH = N_HEADS * HEAD_DIM
IDX_HEADS = 8
IDX_DIM = 64
INDEX_TOPK_MAX = 256
CONV_WIDTH = 512
CONV_GROUPS = 8
CONV_K = 3
N_BRANCHES = 2
D_FF = 2816
FFN_CONV_K = 3
Q_BLOCK = 64
EPS = 1e-6

kernel_name = "hybrid_gated_dsa_shortconv_convffn"

SPLIT_SIZES = (
    ATTN_WIDTH,
    ATTN_WIDTH,
    ATTN_WIDTH,
    IDX_HEADS * IDX_DIM,
    IDX_DIM,
    IDX_HEADS,
    CONV_WIDTH,
    CONV_WIDTH,
    CONV_WIDTH,
    N_BRANCHES * D_MODEL,
)
IN_COLS = sum(SPLIT_SIZES)


def rms_norm(x, g):
    xf = x.astype(jnp.float32)
    y = xf * lax.rsqrt(jnp.mean(xf * xf, axis=-1, keepdims=True) + EPS)
    return (y * g.astype(jnp.float32)).astype(x.dtype)


def causal_dwconv(u, w):
    ksz = w.shape[0]
    s = u.shape[1]
    up = jnp.pad(u, ((0, 0), (ksz - 1, 0), (0, 0)))
    y = up[:, 0:s] * w[0]
    for j in range(1, ksz):
        y = y + up[:, j:j + s] * w[j]
    return y


def sparse_attention(q, k, v, iq, ik, iw):
    b, s = q.shape[0], q.shape[1]
    topk = min(INDEX_TOPK_MAX, s // 4)
    nb = s // Q_BLOCK
    scale = HEAD_DIM ** -0.5
    iw = iw.astype(jnp.float32) * (IDX_HEADS ** -0.5) * (IDX_DIM ** -0.5)
    ikf = ik.astype(jnp.float32)
    key_pos = jnp.arange(s)

    def to_blocks(a):
        return a.reshape((b, nb, Q_BLOCK) + a.shape[2:]).swapaxes(0, 1)

    starts = jnp.arange(nb) * Q_BLOCK

    def block(args):
        qb, iqb, iwb, start = args
        qpos = start + jnp.arange(Q_BLOCK)
        causal = key_pos[None, :] <= qpos[:, None]
        dots = jnp.einsum('bqhd,bsd->bqhs', iqb.astype(jnp.float32), ikf)
        score = jnp.einsum('bqhs,bqh->bqs', jax.nn.relu(dots), iwb)
        score = jnp.where(causal[None], score, -jnp.inf)
        top_val, top_idx = lax.top_k(score, topk)
        valid = jnp.isfinite(top_val)
        k_sel = jax.vmap(lambda kb, ib: kb[ib])(k, top_idx)
        v_sel = jax.vmap(lambda vb, ib: vb[ib])(v, top_idx)
        logits = jnp.einsum('bqhd,bqkhd->bqhk', qb, k_sel).astype(jnp.float32) * scale
        logits = jnp.where(valid[:, :, None, :], logits, -jnp.inf)
        p = jax.nn.softmax(logits, axis=-1).astype(v.dtype)
        return jnp.einsum('bqhk,bqkhd->bqhd', p, v_sel)

    out = lax.map(block, (to_blocks(q), to_blocks(iq), to_blocks(iw), starts))
    return out.swapaxes(0, 1).reshape(b, s, N_HEADS * HEAD_DIM)


def setup_inputs(seed: int = 0) -> dict:
    key = jax.random.key(seed)
    ks = jax.random.split(key, 16)
    f32 = jnp.float32

    def nrm(k, shape, scale):
        return jax.random.normal(k, shape, f32) * scale

    return {
        "x": jax.random.normal(ks[0], (BATCH, SEQ, D_MODEL), f32),
        "norm_mix_g": 1.0 + nrm(ks[1], (DEPTH, D_MODEL), 0.02),
        "w_in": nrm(ks[2], (DEPTH, D_MODEL, IN_COLS), D_MODEL ** -0.5),
        "b_gate": nrm(ks[3], (DEPTH, N_BRANCHES * D_MODEL), 0.02),
        "w_attn_out": nrm(ks[4], (DEPTH, ATTN_WIDTH, D_MODEL), ATTN_WIDTH ** -0.5),
        "conv_w": nrm(ks[5], (DEPTH, CONV_K, CONV_WIDTH), CONV_K ** -0.5),
        "w_conv_out": nrm(ks[6], (DEPTH, CONV_WIDTH, D_MODEL), CONV_WIDTH ** -0.5),
        "w_o": nrm(ks[7], (DEPTH, D_MODEL, D_MODEL), D_MODEL ** -0.5),
        "norm_ffn_g": 1.0 + nrm(ks[8], (DEPTH, D_MODEL), 0.02),
        "w_up": nrm(ks[9], (DEPTH, D_MODEL, 2 * D_FF), D_MODEL ** -0.5),
        "ffn_conv_w": nrm(ks[10], (DEPTH, FFN_CONV_K, 2 * D_FF), FFN_CONV_K ** -0.5),
        "w_down": nrm(ks[11], (DEPTH, D_FF, D_MODEL), D_FF ** -0.5),
        "norm_final_g": 1.0 + nrm(ks[12], (D_MODEL,), 0.02),
    }


def reference(x, norm_mix_g, w_in, b_gate, w_attn_out, conv_w, w_conv_out, w_o,
              norm_ffn_g, w_up, ffn_conv_w, w_down, norm_final_g):
    b, s, _ = x.shape
    split_points = list(np.cumsum(SPLIT_SIZES)[:-1])
    h = x
    for l in range(DEPTH):
        xn = rms_norm(h, norm_mix_g[l])
        proj = jnp.einsum('bsd,dc->bsc', xn, w_in[l])
        q, k, v, iq, ik, iw, ch, cb, cc, gates = jnp.split(proj, split_points, axis=-1)
        q = q.reshape(b, s, N_HEADS, HEAD_DIM)
        k = k.reshape(b, s, N_HEADS, HEAD_DIM)
        v = v.reshape(b, s, N_HEADS, HEAD_DIM)
        iq = iq.reshape(b, s, IDX_HEADS, IDX_DIM)
        attn = sparse_attention(q, k, v, iq, ik, iw)
        branch_a = jnp.einsum('bsc,cd->bsd', attn, w_attn_out[l])
        conv = causal_dwconv(cc * ch, conv_w[l])
        branch_c = jnp.einsum('bsc,cd->bsd', cb * conv, w_conv_out[l])
        g = jax.nn.sigmoid(gates + b_gate[l])
        g_a, g_c = jnp.split(g, N_BRANCHES, axis=-1)
        merged = g_a * branch_a + g_c * branch_c
        h = h + jnp.einsum('bsd,de->bse', merged, w_o[l])
        xn2 = rms_norm(h, norm_ffn_g[l])
        up = jnp.einsum('bsd,df->bsf', xn2, w_up[l])
        up = causal_dwconv(up, ffn_conv_w[l])
        gt, val = jnp.split(up, 2, axis=-1)
        h = h + jnp.einsum('bsf,fd->bsd', jax.nn.silu(gt) * val, w_down[l])
    return rms_norm(h, norm_final_g)
```

```python
import functools

import jax
import jax.numpy as jnp
from jax import lax
from jax.experimental import pallas as pl
from jax.experimental.pallas import tpu as pltpu

D_MODEL = 1024
N_HEADS = 8
HEAD_DIM = 64
ATTN_WIDTH = N_HEADS * HEAD_DIM
IDX_HEADS = 8
IDX_DIM = 64
INDEX_TOPK_MAX = 256
CONV_WIDTH = 512
CONV_K = 3
D_FF = 2816
FFN_CONV_K = 3
EPS = 1e-6

F32 = jnp.float32
BF16 = jnp.bfloat16
I32 = jnp.int32
INT_MIN = -(2 ** 31)

SUBLANES = 8
MXU_DIM = 256

TOKEN_TILE = 256
Q_TILE = MXU_DIM
K_TILE = 256
FFN_CHUNK = 256
VMEM_LIMIT = 56 * 1024 * 1024

_NT = (((1,), (1,)), ((), ()))


def _rms(x, g):
    return x * lax.rsqrt(jnp.mean(x * x, axis=-1, keepdims=True) + EPS) * g


def _shifted_rows(buf_ref, u, first_tile):
    tm = u.shape[0]

    @pl.when(first_tile)
    def _():
        buf_ref[0:SUBLANES, :] = jnp.zeros((SUBLANES, u.shape[1]), F32)

    buf_ref[SUBLANES:SUBLANES + tm, :] = u
    u1 = buf_ref[SUBLANES - 1:SUBLANES - 1 + tm, :]
    u2 = buf_ref[SUBLANES - 2:SUBLANES - 2 + tm, :]
    buf_ref[0:SUBLANES, :] = u[tm - SUBLANES:, :]
    return u1, u2


def _proj_kernel(x_ref, g_ref, wk_ref, wik_ref, wconv_ref, wg_ref,
                 wqT_ref, wvT_ref, wiqT_ref, wiwT_ref,
                 bg_ref, cw_ref, wco_ref,
                 k_ref, ik_ref, qT_ref, vT_ref, iqT_ref, iwT_ref, ga_ref, mc_ref,
                 ubuf_ref):
    j = pl.program_id(1)
    xn = _rms(x_ref[0], g_ref[...]).astype(BF16)

    qT = lax.dot_general(wqT_ref[...], xn, _NT, preferred_element_type=F32)
    qT_ref[0] = (qT * (HEAD_DIM ** -0.5)).astype(BF16)
    vT = lax.dot_general(wvT_ref[...], xn, _NT, preferred_element_type=F32)
    vT_ref[0] = vT.astype(BF16)
    iqT_ref[0] = lax.dot_general(wiqT_ref[...], xn, _NT, preferred_element_type=F32)
    iwT = lax.dot_general(wiwT_ref[...], xn, _NT, preferred_element_type=F32)
    iwT_ref[0] = iwT[0:IDX_HEADS, :]

    kk = jnp.dot(xn, wk_ref[...], preferred_element_type=F32)
    for h in range(N_HEADS):
        k_ref[0, h] = kk[:, h * HEAD_DIM:(h + 1) * HEAD_DIM].astype(BF16)
    ik = jnp.dot(xn, wik_ref[...], preferred_element_type=F32)
    ik_ref[0] = ik[:, 0:IDX_DIM]

    pc = jnp.dot(xn, wconv_ref[...], preferred_element_type=F32)
    ch = pc[:, 0:CONV_WIDTH]
    cb = pc[:, CONV_WIDTH:2 * CONV_WIDTH]
    cc = pc[:, 2 * CONV_WIDTH:3 * CONV_WIDTH]
    u = cc * ch
    u1, u2 = _shifted_rows(ubuf_ref, u, j == 0)
    cw = cw_ref[...]
    conv = u2 * cw[0:1, :] + u1 * cw[1:2, :] + u * cw[2:3, :]
    bc = jnp.dot((cb * conv).astype(BF16), wco_ref[...],
                 preferred_element_type=F32)

    g = jax.nn.sigmoid(jnp.dot(xn, wg_ref[...], preferred_element_type=F32)
                       + bg_ref[...])
    ga_ref[0] = g[:, 0:D_MODEL]
    mc_ref[0] = g[:, D_MODEL:2 * D_MODEL] * bc


def _attn_kernel(iqT_ref, iwT_ref, qT_ref, ik_ref, k_ref, vT_ref, o_ref,
                 keys_ref, bias_ref, l_ref, oT_ref, tlim_ref, *, seq_len, topk):
    i = pl.program_id(1)
    nch = i + 1
    row_k = lax.broadcasted_iota(I32, (K_TILE, Q_TILE), 0)
    lane_q = lax.broadcasted_iota(I32, (K_TILE, Q_TILE), 1)
    rel = row_k - lane_q

    def chunk_start(c):
        return pl.multiple_of(c * K_TILE, K_TILE)

    def fold_rows(m):
        return jnp.sum(m.reshape(K_TILE // SUBLANES, SUBLANES, Q_TILE), axis=0)

    iw = iwT_ref[0] * ((IDX_HEADS ** -0.5) * (IDX_DIM ** -0.5))

    def score_chunk(c, carry):
        r0 = chunk_start(c)
        ikc = ik_ref[0, pl.ds(r0, K_TILE), :].astype(BF16)
        acc = jnp.zeros((K_TILE, Q_TILE), F32)
        for h in range(IDX_HEADS):
            iq_h = iqT_ref[0, h * IDX_DIM:(h + 1) * IDX_DIM, :].astype(BF16)
            d = jnp.dot(ikc, iq_h, preferred_element_type=F32)
            acc = acc + jnp.maximum(d, 0.0) * iw[h:h + 1, :]
        acc = jnp.where(acc == 0.0, 0.0, acc)
        bits = pltpu.bitcast(acc, I32)
        key = bits ^ ((bits >> 31) & 0x7FFFFFFF)
        valid = rel <= (i - c) * K_TILE
        keys_ref[pl.ds(r0, K_TILE), :] = jnp.where(valid, key, INT_MIN)
        return carry

    lax.fori_loop(0, nch, score_chunk, 0)

    def count_where(pred):
        def body(c, acc):
            r0 = chunk_start(c)
            kc = keys_ref[pl.ds(r0, K_TILE), :]
            return acc + fold_rows(jnp.where(pred(kc, r0), 1, 0).astype(I32))
        acc = lax.fori_loop(0, nch, body, jnp.zeros((SUBLANES, Q_TILE), I32))
        return jnp.sum(acc, axis=0, keepdims=True)

    def radix_step(t, thr):
        bit = lax.shift_left(jnp.int32(1), 31 - t)
        cand = thr ^ bit
        cnt = count_where(lambda kc, r0: kc >= cand)
        return jnp.where(cnt >= topk, cand, thr)

    thr = lax.fori_loop(0, 32, radix_step, jnp.full((1, Q_TILE), INT_MIN, I32))
    thr = jnp.maximum(thr, INT_MIN + 1)
    cnt_gt = count_where(lambda kc, r0: kc > thr)
    cnt_eq = count_where(lambda kc, r0: kc == thr)
    need = topk - cnt_gt

    tlim_ref[...] = jnp.full((SUBLANES, Q_TILE), seq_len, I32)

    @pl.when(jnp.max(cnt_eq - need) > 0)
    def _():
        def tie_step(t, lim):
            bit = lax.shift_left(jnp.int32(1), (seq_len.bit_length() - 2) - t)
            cand = lim | bit
            cnt = count_where(
                lambda kc, r0: jnp.logical_and(kc == thr, (row_k + r0) < cand))
            return jnp.where(cnt < need, cand, lim)
        lim = lax.fori_loop(0, seq_len.bit_length() - 1, tie_step,
                            jnp.zeros((1, Q_TILE), I32))
        tlim_ref[...] = jnp.broadcast_to(lim, (SUBLANES, Q_TILE))

    tlim = tlim_ref[0:1, :]

    def bias_chunk(c, carry):
        r0 = chunk_start(c)
        kc = keys_ref[pl.ds(r0, K_TILE), :]
        tie_ok = jnp.logical_and(kc == thr, (row_k + r0) <= tlim)
        sel = jnp.logical_or(kc > thr, tie_ok)
        bias_ref[pl.ds(r0, K_TILE), :] = jnp.where(sel, 0.0, -jnp.inf).astype(F32)
        return carry

    lax.fori_loop(0, nch, bias_chunk, 0)

    for h in range(N_HEADS):
        q_h = qT_ref[0, h * HEAD_DIM:(h + 1) * HEAD_DIM, :]

        def logits_chunk(c, m, h=h, q_h=q_h):
            r0 = chunk_start(c)
            kc = k_ref[0, h, pl.ds(r0, K_TILE), :]
            l = jnp.dot(kc, q_h, preferred_element_type=F32) + bias_ref[pl.ds(r0, K_TILE), :]
            l_ref[pl.ds(r0, K_TILE), :] = l
            return jnp.maximum(
                m, jnp.max(l.reshape(K_TILE // SUBLANES, SUBLANES, Q_TILE), axis=0))

        m8 = lax.fori_loop(0, nch, logits_chunk,
                           jnp.full((SUBLANES, Q_TILE), -jnp.inf, F32))
        m = jnp.max(m8, axis=0, keepdims=True)

        def pv_chunk(c, carry, h=h, m=m):
            s8, o = carry
            r0 = chunk_start(c)
            p = jnp.exp(l_ref[pl.ds(r0, K_TILE), :] - m)
            s8 = s8 + fold_rows(p)
            v_c = vT_ref[0, h * HEAD_DIM:(h + 1) * HEAD_DIM, pl.ds(r0, K_TILE)]
            o = o + jnp.dot(v_c, p.astype(BF16), preferred_element_type=F32)
            return s8, o

        s8, o = lax.fori_loop(
            0, nch, pv_chunk,
            (jnp.zeros((SUBLANES, Q_TILE), F32), jnp.zeros((HEAD_DIM, Q_TILE), F32)))
        denom = jnp.sum(s8, axis=0, keepdims=True)
        oT_ref[h * HEAD_DIM:(h + 1) * HEAD_DIM, :] = o / denom

    o_ref[0] = oT_ref[...].T.astype(BF16)


def _ffn_kernel(x_ref, attn_ref, ga_ref, mc_ref, wao_ref, wo_ref, g2_ref,
                wup_ref, fcw_ref, wdn_ref, gf_ref, out_ref,
                gbuf_ref, vbuf_ref, gcar_ref, vcar_ref):
    j = pl.program_id(1)
    tm = x_ref.shape[1]
    ba = jnp.dot(attn_ref[0], wao_ref[...], preferred_element_type=F32)
    merged = (ga_ref[0] * ba + mc_ref[0]).astype(BF16)
    h1 = x_ref[0] + jnp.dot(merged, wo_ref[...], preferred_element_type=F32)
    xn2 = _rms(h1, g2_ref[...]).astype(BF16)

    @pl.when(j == 0)
    def _():
        gcar_ref[...] = jnp.zeros(gcar_ref.shape, F32)
        vcar_ref[...] = jnp.zeros(vcar_ref.shape, F32)

    def causal_conv(u, buf_ref, car_ref, n, col0):
        buf_ref[0:SUBLANES, :] = car_ref[n]
        buf_ref[SUBLANES:SUBLANES + tm, :] = u
        u1 = buf_ref[SUBLANES - 1:SUBLANES - 1 + tm, :]
        u2 = buf_ref[SUBLANES - 2:SUBLANES - 2 + tm, :]
        car_ref[n] = u[tm - SUBLANES:, :]
        w = fcw_ref[:, col0:col0 + FFN_CHUNK]
        return u2 * w[0:1, :] + u1 * w[1:2, :] + u * w[2:3, :]

    acc = jnp.zeros((tm, D_MODEL), F32)
    for n in range(D_FF // FFN_CHUNK):
        c0 = n * FFN_CHUNK
        ug = jnp.dot(xn2, wup_ref[:, c0:c0 + FFN_CHUNK], preferred_element_type=F32)
        uv = jnp.dot(xn2, wup_ref[:, D_FF + c0:D_FF + c0 + FFN_CHUNK],
                     preferred_element_type=F32)
        gt = causal_conv(ug, gbuf_ref, gcar_ref, n, c0)
        val = causal_conv(uv, vbuf_ref, vcar_ref, n, D_FF + c0)
        act = (gt * jax.nn.sigmoid(gt) * val).astype(BF16)
        acc = acc + jnp.dot(act, wdn_ref[c0:c0 + FFN_CHUNK, :], preferred_element_type=F32)

    out_ref[0] = _rms(h1 + acc, gf_ref[...])


def _const_spec(shape):
    nd = len(shape)
    return pl.BlockSpec(shape, lambda b, j: (0,) * nd, pipeline_mode=pl.Buffered(1))


def kernel(x, norm_mix_g, w_in, b_gate, w_attn_out, conv_w, w_conv_out, w_o,
           norm_ffn_g, w_up, ffn_conv_w, w_down, norm_final_g):
    B, S, D = x.shape
    assert D == D_MODEL and S % TOKEN_TILE == 0 and S % Q_TILE == 0 and Q_TILE == K_TILE
    assert norm_mix_g.shape[0] == 1, "single-layer block"
    topk = min(INDEX_TOPK_MAX, S // 4)
    tm = TOKEN_TILE
    nt = S // tm

    w = w_in[0]
    o_q, o_k, o_v, o_iq = 0, ATTN_WIDTH, 2 * ATTN_WIDTH, 3 * ATTN_WIDTH
    o_ik = o_iq + IDX_HEADS * IDX_DIM
    o_iw = o_ik + IDX_DIM
    o_ch = o_iw + IDX_HEADS
    o_g = o_ch + 3 * CONV_WIDTH
    wqT = w[:, o_q:o_k].T.astype(BF16)
    wvT = w[:, o_v:o_iq].T.astype(BF16)
    wiqT = w[:, o_iq:o_ik].T.astype(BF16)
    wiwT = jnp.pad(w[:, o_iw:o_ch].T, ((0, 16 - IDX_HEADS), (0, 0))).astype(BF16)
    wk = w[:, o_k:o_v].astype(BF16)
    wik = jnp.pad(w[:, o_ik:o_iw], ((0, 0), (0, 128 - IDX_DIM))).astype(BF16)
    wconv = w[:, o_ch:o_g].astype(BF16)
    wg = w[:, o_g:].astype(BF16)
    g1 = norm_mix_g.reshape(1, D)
    bg = b_gate.reshape(1, 2 * D)
    cw = conv_w[0]
    wco = w_conv_out[0].astype(BF16)

    cparams = pltpu.CompilerParams(
        dimension_semantics=("arbitrary", "arbitrary"), vmem_limit_bytes=VMEM_LIMIT)

    tok = lambda c: pl.BlockSpec((1, tm, c), lambda b, j: (b, j, 0))
    featT = lambda c, t: pl.BlockSpec((1, c, t), lambda b, j: (b, 0, j))

    k, ik, qT, vT, iqT, iwT, ga, mc = pl.pallas_call(
        _proj_kernel,
        grid=(B, nt),
        in_specs=[tok(D), _const_spec((1, D)),
                  _const_spec(wk.shape), _const_spec(wik.shape),
                  _const_spec(wconv.shape), _const_spec(wg.shape),
                  _const_spec(wqT.shape), _const_spec(wvT.shape),
                  _const_spec(wiqT.shape), _const_spec(wiwT.shape),
                  _const_spec(bg.shape), _const_spec(cw.shape), _const_spec(wco.shape)],
        out_specs=[pl.BlockSpec((1, N_HEADS, tm, HEAD_DIM), lambda b, j: (b, 0, j, 0)),
                   tok(IDX_DIM), featT(ATTN_WIDTH, tm), featT(ATTN_WIDTH, tm),
                   featT(IDX_HEADS * IDX_DIM, tm), featT(IDX_HEADS, tm),
                   tok(D), tok(D)],
        out_shape=[jax.ShapeDtypeStruct((B, N_HEADS, S, HEAD_DIM), BF16),
                   jax.ShapeDtypeStruct((B, S, IDX_DIM), F32),
                   jax.ShapeDtypeStruct((B, ATTN_WIDTH, S), BF16),
                   jax.ShapeDtypeStruct((B, ATTN_WIDTH, S), BF16),
                   jax.ShapeDtypeStruct((B, IDX_HEADS * IDX_DIM, S), F32),
                   jax.ShapeDtypeStruct((B, IDX_HEADS, S), F32),
                   jax.ShapeDtypeStruct((B, S, D), F32),
                   jax.ShapeDtypeStruct((B, S, D), F32)],
        scratch_shapes=[pltpu.VMEM((tm + SUBLANES, CONV_WIDTH), F32)],
        compiler_params=cparams,
        name="proj",
    )(x, g1, wk, wik, wconv, wg, wqT, wvT, wiqT, wiwT, bg, cw, wco)

    nq = S // Q_TILE
    attn = pl.pallas_call(
        functools.partial(_attn_kernel, seq_len=S, topk=topk),
        grid=(B, nq),
        in_specs=[featT(IDX_HEADS * IDX_DIM, Q_TILE), featT(IDX_HEADS, Q_TILE),
                  featT(ATTN_WIDTH, Q_TILE),
                  pl.BlockSpec((1, S, IDX_DIM), lambda b, j: (b, 0, 0)),
                  pl.BlockSpec((1, N_HEADS, S, HEAD_DIM), lambda b, j: (b, 0, 0, 0)),
                  pl.BlockSpec((1, ATTN_WIDTH, S), lambda b, j: (b, 0, 0))],
        out_specs=pl.BlockSpec((1, Q_TILE, ATTN_WIDTH), lambda b, j: (b, j, 0)),
        out_shape=jax.ShapeDtypeStruct((B, S, ATTN_WIDTH), BF16),
        scratch_shapes=[pltpu.VMEM((S, Q_TILE), I32),
                        pltpu.VMEM((S, Q_TILE), F32),
                        pltpu.VMEM((S, Q_TILE), F32),
                        pltpu.VMEM((ATTN_WIDTH, Q_TILE), F32),
                        pltpu.VMEM((SUBLANES, Q_TILE), I32)],
        compiler_params=cparams,
        name="attn",
    )(iqT, iwT, qT, ik, k, vT)

    wao = w_attn_out[0].astype(BF16)
    wo = w_o[0].astype(BF16)
    wup = w_up[0].astype(BF16)
    wdn = w_down[0].astype(BF16)
    fcw = ffn_conv_w[0]
    g2 = norm_ffn_g.reshape(1, D)
    gf = norm_final_g.reshape(1, D)
    nchunk = D_FF // FFN_CHUNK
    assert nchunk * FFN_CHUNK == D_FF

    out = pl.pallas_call(
        _ffn_kernel,
        grid=(B, nt),
        in_specs=[tok(D), tok(ATTN_WIDTH), tok(D), tok(D),
                  _const_spec(wao.shape), _const_spec(wo.shape), _const_spec(g2.shape),
                  _const_spec(wup.shape), _const_spec(fcw.shape), _const_spec(wdn.shape),
                  _const_spec(gf.shape)],
        out_specs=tok(D),
        out_shape=jax.ShapeDtypeStruct((B, S, D), F32),
        scratch_shapes=[pltpu.VMEM((tm + SUBLANES, FFN_CHUNK), F32),
                        pltpu.VMEM((tm + SUBLANES, FFN_CHUNK), F32),
                        pltpu.VMEM((nchunk, SUBLANES, FFN_CHUNK), F32),
                        pltpu.VMEM((nchunk, SUBLANES, FFN_CHUNK), F32)],
        compiler_params=cparams,
        name="ffn",
    )(x, attn, ga, mc, wao, wo, g2, wup, fcw, wdn, gf)
    return out
```

```python
import functools

import jax
import jax.numpy as jnp
from jax import lax
from jax.experimental import pallas as pl
from jax.experimental.pallas import tpu as pltpu

D_MODEL = 1024
N_HEADS = 8
HEAD_DIM = 64
ATTN_WIDTH = N_HEADS * HEAD_DIM
IDX_HEADS = 8
IDX_DIM = 64
INDEX_TOPK_MAX = 256
CONV_WIDTH = 512
CONV_K = 3
D_FF = 2816
FFN_CONV_K = 3
EPS = 1e-6

F32 = jnp.float32
BF16 = jnp.bfloat16
I32 = jnp.int32
INT_MIN = -(2 ** 31)

SUBLANES = 8
MXU_DIM = 256

TOKEN_TILE = 256
Q_TILE = MXU_DIM
K_TILE = 256
SCORE_ROWS = 64
FFN_CHUNK = 256
VMEM_LIMIT = 56 * 1024 * 1024

_NT = (((1,), (1,)), ((), ()))


def _rms(x, g):
    return x * lax.rsqrt(jnp.mean(x * x, axis=-1, keepdims=True) + EPS) * g


def _shifted_rows(buf_ref, u, first_tile):
    tm = u.shape[0]

    @pl.when(first_tile)
    def _():
        buf_ref[0:SUBLANES, :] = jnp.zeros((SUBLANES, u.shape[1]), F32)

    buf_ref[SUBLANES:SUBLANES + tm, :] = u
    u1 = buf_ref[SUBLANES - 1:SUBLANES - 1 + tm, :]
    u2 = buf_ref[SUBLANES - 2:SUBLANES - 2 + tm, :]
    buf_ref[0:SUBLANES, :] = u[tm - SUBLANES:, :]
    return u1, u2


def _proj_kernel(x_ref, g_ref, wk_ref, wik_ref, wconv_ref, wg_ref,
                 wqT_ref, wvT_ref, wiqT_ref, wiwT_ref,
                 bg_ref, cw_ref, wco_ref,
                 k_ref, ik_ref, qT_ref, vT_ref, iqT_ref, iwT_ref, ga_ref, mc_ref,
                 ubuf_ref):
    j = pl.program_id(1)
    xn = _rms(x_ref[0], g_ref[...]).astype(BF16)

    qT = lax.dot_general(wqT_ref[...], xn, _NT, preferred_element_type=F32)
    qT_ref[0] = (qT * (HEAD_DIM ** -0.5)).astype(BF16)
    vT = lax.dot_general(wvT_ref[...], xn, _NT, preferred_element_type=F32)
    vT_ref[0] = vT.astype(BF16)
    iqT_ref[0] = lax.dot_general(wiqT_ref[...], xn, _NT, preferred_element_type=F32)
    iwT = lax.dot_general(wiwT_ref[...], xn, _NT, preferred_element_type=F32)
    iwT_ref[0] = iwT[0:IDX_HEADS, :]

    kk = jnp.dot(xn, wk_ref[...], preferred_element_type=F32)
    for h in range(N_HEADS):
        k_ref[0, h] = kk[:, h * HEAD_DIM:(h + 1) * HEAD_DIM].astype(BF16)
    ik = jnp.dot(xn, wik_ref[...], preferred_element_type=F32)
    ik_ref[0] = ik[:, 0:IDX_DIM]

    pc = jnp.dot(xn, wconv_ref[...], preferred_element_type=F32)
    ch = pc[:, 0:CONV_WIDTH]
    cb = pc[:, CONV_WIDTH:2 * CONV_WIDTH]
    cc = pc[:, 2 * CONV_WIDTH:3 * CONV_WIDTH]
    u = cc * ch
    u1, u2 = _shifted_rows(ubuf_ref, u, j == 0)
    cw = cw_ref[...]
    conv = u2 * cw[0:1, :] + u1 * cw[1:2, :] + u * cw[2:3, :]
    bc = jnp.dot((cb * conv).astype(BF16), wco_ref[...],
                 preferred_element_type=F32)

    g = jax.nn.sigmoid(jnp.dot(xn, wg_ref[...], preferred_element_type=F32)
                       + bg_ref[...])
    ga_ref[0] = g[:, 0:D_MODEL]
    mc_ref[0] = g[:, D_MODEL:2 * D_MODEL] * bc


def _attn_kernel(iqT_ref, iwT_ref, qT_ref, ik_ref, k_ref, vT_ref, o_ref,
                 keys_ref, bias_ref, l_ref, oT_ref, tlim_ref, *, seq_len, topk):
    i = pl.program_id(1)
    nch = i + 1
    row_k = lax.broadcasted_iota(I32, (K_TILE, Q_TILE), 0)
    lane_q = lax.broadcasted_iota(I32, (K_TILE, Q_TILE), 1)
    rel = row_k - lane_q

    def chunk_start(c):
        return pl.multiple_of(c * K_TILE, K_TILE)

    def fold_rows(m):
        return jnp.sum(m.reshape(K_TILE // SUBLANES, SUBLANES, Q_TILE), axis=0)

    iw = iwT_ref[0] * ((IDX_HEADS ** -0.5) * (IDX_DIM ** -0.5))

    def score_chunk(c, carry):
        for sb in range(K_TILE // SCORE_ROWS):
            r0 = pl.multiple_of(c * K_TILE + sb * SCORE_ROWS, SCORE_ROWS)
            ikc = ik_ref[0, pl.ds(r0, SCORE_ROWS), :].astype(BF16)
            acc = jnp.zeros((SCORE_ROWS, Q_TILE), F32)
            for h in range(IDX_HEADS):
                iq_h = iqT_ref[0, h * IDX_DIM:(h + 1) * IDX_DIM, :].astype(BF16)
                d = jnp.dot(ikc, iq_h, preferred_element_type=F32)
                acc = acc + jnp.maximum(d, 0.0) * iw[h:h + 1, :]
            acc = jnp.where(acc == 0.0, 0.0, acc)
            bits = pltpu.bitcast(acc, I32)
            key = bits ^ ((bits >> 31) & 0x7FFFFFFF)
            valid = (rel[0:SCORE_ROWS, :] + sb * SCORE_ROWS) <= (i - c) * K_TILE
            keys_ref[pl.ds(r0, SCORE_ROWS), :] = jnp.where(valid, key, INT_MIN)
        return carry

    lax.fori_loop(0, nch, score_chunk, 0)

    def count_where(pred):
        def body(c, acc):
            r0 = chunk_start(c)
            kc = keys_ref[pl.ds(r0, K_TILE), :]
            return acc + fold_rows(jnp.where(pred(kc, r0), 1, 0).astype(I32))
        acc = lax.fori_loop(0, nch, body, jnp.zeros((SUBLANES, Q_TILE), I32))
        return jnp.sum(acc, axis=0, keepdims=True)

    def radix_step(t, thr):
        bit = lax.shift_left(jnp.int32(1), 31 - t)
        cand = thr ^ bit
        cnt = count_where(lambda kc, r0: kc >= cand)
        return jnp.where(cnt >= topk, cand, thr)

    thr = lax.fori_loop(0, 32, radix_step, jnp.full((1, Q_TILE), INT_MIN, I32))
    thr = jnp.maximum(thr, INT_MIN + 1)
    cnt_gt = count_where(lambda kc, r0: kc > thr)
    cnt_eq = count_where(lambda kc, r0: kc == thr)
    need = topk - cnt_gt

    tlim_ref[...] = jnp.full((SUBLANES, Q_TILE), seq_len, I32)

    @pl.when(jnp.max(cnt_eq - need) > 0)
    def _():
        def tie_step(t, lim):
            bit = lax.shift_left(jnp.int32(1), (seq_len.bit_length() - 2) - t)
            cand = lim | bit
            cnt = count_where(
                lambda kc, r0: jnp.logical_and(kc == thr, (row_k + r0) < cand))
            return jnp.where(cnt < need, cand, lim)
        lim = lax.fori_loop(0, seq_len.bit_length() - 1, tie_step,
                            jnp.zeros((1, Q_TILE), I32))
        tlim_ref[...] = jnp.broadcast_to(lim, (SUBLANES, Q_TILE))

    tlim = tlim_ref[0:1, :]

    def logits_chunk(c, m8):
        r0 = chunk_start(c)
        kc = keys_ref[pl.ds(r0, K_TILE), :]
        tie_ok = jnp.logical_and(kc == thr, (row_k + r0) <= tlim)
        sel = jnp.logical_or(kc > thr, tie_ok)
        bias_ref[...] = jnp.where(sel, 0.0, -jnp.inf).astype(F32)
        out = []
        for h in range(N_HEADS):
            hs = slice(h * HEAD_DIM, (h + 1) * HEAD_DIM)
            l = jnp.dot(k_ref[0, h, pl.ds(r0, K_TILE), :], qT_ref[0, hs, :],
                        preferred_element_type=F32) + bias_ref[...]
            l_ref[h, pl.ds(r0, K_TILE), :] = l
            out.append(jnp.maximum(
                m8[h], jnp.max(l.reshape(K_TILE // SUBLANES, SUBLANES, Q_TILE), axis=0)))
        return tuple(out)

    m8 = lax.fori_loop(
        0, nch, logits_chunk,
        tuple(jnp.full((SUBLANES, Q_TILE), -jnp.inf, F32) for _ in range(N_HEADS)))
    m_all = [jnp.max(m, axis=0, keepdims=True) for m in m8]

    oT_ref[...] = jnp.zeros(oT_ref.shape, F32)

    def pv_chunk(c, s8):
        r0 = chunk_start(c)
        out = []
        for h in range(N_HEADS):
            hs = slice(h * HEAD_DIM, (h + 1) * HEAD_DIM)
            p = jnp.exp(l_ref[h, pl.ds(r0, K_TILE), :] - m_all[h])
            out.append(s8[h] + fold_rows(p))
            oT_ref[hs, :] += jnp.dot(vT_ref[0, hs, pl.ds(r0, K_TILE)], p.astype(BF16),
                                     preferred_element_type=F32)
        return tuple(out)

    s8 = lax.fori_loop(
        0, nch, pv_chunk,
        tuple(jnp.zeros((SUBLANES, Q_TILE), F32) for _ in range(N_HEADS)))
    for h in range(N_HEADS):
        hs = slice(h * HEAD_DIM, (h + 1) * HEAD_DIM)
        oT_ref[hs, :] = oT_ref[hs, :] / jnp.sum(s8[h], axis=0, keepdims=True)

    o_ref[0] = oT_ref[...].T.astype(BF16)


def _ffn_kernel(x_ref, attn_ref, ga_ref, mc_ref, wao_ref, wo_ref, g2_ref,
                wup_ref, fcw_ref, wdn_ref, gf_ref, out_ref,
                gbuf_ref, vbuf_ref, gcar_ref, vcar_ref):
    j = pl.program_id(1)
    tm = x_ref.shape[1]
    ba = jnp.dot(attn_ref[0], wao_ref[...], preferred_element_type=F32)
    merged = (ga_ref[0] * ba + mc_ref[0]).astype(BF16)
    h1 = x_ref[0] + jnp.dot(merged, wo_ref[...], preferred_element_type=F32)
    xn2 = _rms(h1, g2_ref[...]).astype(BF16)

    @pl.when(j == 0)
    def _():
        gcar_ref[...] = jnp.zeros(gcar_ref.shape, F32)
        vcar_ref[...] = jnp.zeros(vcar_ref.shape, F32)

    def causal_conv(u, buf_ref, car_ref, n, col0):
        buf_ref[0:SUBLANES, :] = car_ref[n]
        buf_ref[SUBLANES:SUBLANES + tm, :] = u
        u1 = buf_ref[SUBLANES - 1:SUBLANES - 1 + tm, :]
        u2 = buf_ref[SUBLANES - 2:SUBLANES - 2 + tm, :]
        car_ref[n] = u[tm - SUBLANES:, :]
        w = fcw_ref[:, col0:col0 + FFN_CHUNK]
        return u2 * w[0:1, :] + u1 * w[1:2, :] + u * w[2:3, :]

    acc = jnp.zeros((tm, D_MODEL), F32)
    for n in range(D_FF // FFN_CHUNK):
        c0 = n * FFN_CHUNK
        ug = jnp.dot(xn2, wup_ref[:, c0:c0 + FFN_CHUNK], preferred_element_type=F32)
        uv = jnp.dot(xn2, wup_ref[:, D_FF + c0:D_FF + c0 + FFN_CHUNK],
                     preferred_element_type=F32)
        gt = causal_conv(ug, gbuf_ref, gcar_ref, n, c0)
        val = causal_conv(uv, vbuf_ref, vcar_ref, n, D_FF + c0)
        act = (gt * jax.nn.sigmoid(gt) * val).astype(BF16)
        acc = acc + jnp.dot(act, wdn_ref[c0:c0 + FFN_CHUNK, :], preferred_element_type=F32)

    out_ref[0] = _rms(h1 + acc, gf_ref[...])


def _const_spec(shape):
    nd = len(shape)
    return pl.BlockSpec(shape, lambda b, j: (0,) * nd, pipeline_mode=pl.Buffered(1))


def kernel(x, norm_mix_g, w_in, b_gate, w_attn_out, conv_w, w_conv_out, w_o,
           norm_ffn_g, w_up, ffn_conv_w, w_down, norm_final_g):
    B, S, D = x.shape
    assert D == D_MODEL and S % TOKEN_TILE == 0 and S % Q_TILE == 0 and Q_TILE == K_TILE
    assert norm_mix_g.shape[0] == 1, "single-layer block"
    topk = min(INDEX_TOPK_MAX, S // 4)
    tm = TOKEN_TILE
    nt = S // tm

    w = w_in[0]
    o_q, o_k, o_v, o_iq = 0, ATTN_WIDTH, 2 * ATTN_WIDTH, 3 * ATTN_WIDTH
    o_ik = o_iq + IDX_HEADS * IDX_DIM
    o_iw = o_ik + IDX_DIM
    o_ch = o_iw + IDX_HEADS
    o_g = o_ch + 3 * CONV_WIDTH
    wqT = w[:, o_q:o_k].T.astype(BF16)
    wvT = w[:, o_v:o_iq].T.astype(BF16)
    wiqT = w[:, o_iq:o_ik].T.astype(BF16)
    wiwT = jnp.pad(w[:, o_iw:o_ch].T, ((0, 16 - IDX_HEADS), (0, 0))).astype(BF16)
    wk = w[:, o_k:o_v].astype(BF16)
    wik = jnp.pad(w[:, o_ik:o_iw], ((0, 0), (0, 128 - IDX_DIM))).astype(BF16)
    wconv = w[:, o_ch:o_g].astype(BF16)
    wg = w[:, o_g:].astype(BF16)
    g1 = norm_mix_g.reshape(1, D)
    bg = b_gate.reshape(1, 2 * D)
    cw = conv_w[0]
    wco = w_conv_out[0].astype(BF16)

    cparams = pltpu.CompilerParams(
        dimension_semantics=("arbitrary", "arbitrary"), vmem_limit_bytes=VMEM_LIMIT)

    tok = lambda c: pl.BlockSpec((1, tm, c), lambda b, j: (b, j, 0))
    featT = lambda c, t: pl.BlockSpec((1, c, t), lambda b, j: (b, 0, j))

    k, ik, qT, vT, iqT, iwT, ga, mc = pl.pallas_call(
        _proj_kernel,
        grid=(B, nt),
        in_specs=[tok(D), _const_spec((1, D)),
                  _const_spec(wk.shape), _const_spec(wik.shape),
                  _const_spec(wconv.shape), _const_spec(wg.shape),
                  _const_spec(wqT.shape), _const_spec(wvT.shape),
                  _const_spec(wiqT.shape), _const_spec(wiwT.shape),
                  _const_spec(bg.shape), _const_spec(cw.shape), _const_spec(wco.shape)],
        out_specs=[pl.BlockSpec((1, N_HEADS, tm, HEAD_DIM), lambda b, j: (b, 0, j, 0)),
                   tok(IDX_DIM), featT(ATTN_WIDTH, tm), featT(ATTN_WIDTH, tm),
                   featT(IDX_HEADS * IDX_DIM, tm), featT(IDX_HEADS, tm),
                   tok(D), tok(D)],
        out_shape=[jax.ShapeDtypeStruct((B, N_HEADS, S, HEAD_DIM), BF16),
                   jax.ShapeDtypeStruct((B, S, IDX_DIM), F32),
                   jax.ShapeDtypeStruct((B, ATTN_WIDTH, S), BF16),
                   jax.ShapeDtypeStruct((B, ATTN_WIDTH, S), BF16),
                   jax.ShapeDtypeStruct((B, IDX_HEADS * IDX_DIM, S), F32),
                   jax.ShapeDtypeStruct((B, IDX_HEADS, S), F32),
                   jax.ShapeDtypeStruct((B, S, D), F32),
                   jax.ShapeDtypeStruct((B, S, D), F32)],
        scratch_shapes=[pltpu.VMEM((tm + SUBLANES, CONV_WIDTH), F32)],
        compiler_params=cparams,
        name="proj",
    )(x, g1, wk, wik, wconv, wg, wqT, wvT, wiqT, wiwT, bg, cw, wco)

    nq = S // Q_TILE
    attn = pl.pallas_call(
        functools.partial(_attn_kernel, seq_len=S, topk=topk),
        grid=(B, nq),
        in_specs=[featT(IDX_HEADS * IDX_DIM, Q_TILE), featT(IDX_HEADS, Q_TILE),
                  featT(ATTN_WIDTH, Q_TILE),
                  pl.BlockSpec((1, S, IDX_DIM), lambda b, j: (b, 0, 0)),
                  pl.BlockSpec((1, N_HEADS, S, HEAD_DIM), lambda b, j: (b, 0, 0, 0)),
                  pl.BlockSpec((1, ATTN_WIDTH, S), lambda b, j: (b, 0, 0))],
        out_specs=pl.BlockSpec((1, Q_TILE, ATTN_WIDTH), lambda b, j: (b, j, 0)),
        out_shape=jax.ShapeDtypeStruct((B, S, ATTN_WIDTH), BF16),
        scratch_shapes=[pltpu.VMEM((S, Q_TILE), I32),
                        pltpu.VMEM((K_TILE, Q_TILE), F32),
                        pltpu.VMEM((N_HEADS, S, Q_TILE), F32),
                        pltpu.VMEM((ATTN_WIDTH, Q_TILE), F32),
                        pltpu.VMEM((SUBLANES, Q_TILE), I32)],
        compiler_params=cparams,
        name="attn",
    )(iqT, iwT, qT, ik, k, vT)

    wao = w_attn_out[0].astype(BF16)
    wo = w_o[0].astype(BF16)
    wup = w_up[0].astype(BF16)
    wdn = w_down[0].astype(BF16)
    fcw = ffn_conv_w[0]
    g2 = norm_ffn_g.reshape(1, D)
    gf = norm_final_g.reshape(1, D)
    nchunk = D_FF // FFN_CHUNK
    assert nchunk * FFN_CHUNK == D_FF

    out = pl.pallas_call(
        _ffn_kernel,
        grid=(B, nt),
        in_specs=[tok(D), tok(ATTN_WIDTH), tok(D), tok(D),
                  _const_spec(wao.shape), _const_spec(wo.shape), _const_spec(g2.shape),
                  _const_spec(wup.shape), _const_spec(fcw.shape), _const_spec(wdn.shape),
                  _const_spec(gf.shape)],
        out_specs=tok(D),
        out_shape=jax.ShapeDtypeStruct((B, S, D), F32),
        scratch_shapes=[pltpu.VMEM((tm + SUBLANES, FFN_CHUNK), F32),
                        pltpu.VMEM((tm + SUBLANES, FFN_CHUNK), F32),
                        pltpu.VMEM((nchunk, SUBLANES, FFN_CHUNK), F32),
                        pltpu.VMEM((nchunk, SUBLANES, FFN_CHUNK), F32)],
        compiler_params=cparams,
        name="ffn",
    )(x, attn, ga, mc, wao, wo, g2, wup, fcw, wdn, gf)
    return out
```

```python
import functools

import jax
import jax.numpy as jnp
from jax import lax
from jax.experimental import pallas as pl
from jax.experimental.pallas import tpu as pltpu

D_MODEL = 1024
N_HEADS = 8
HEAD_DIM = 64
ATTN_WIDTH = N_HEADS * HEAD_DIM
IDX_HEADS = 8
IDX_DIM = 64
INDEX_TOPK_MAX = 256
CONV_WIDTH = 512
CONV_K = 3
D_FF = 2816
FFN_CONV_K = 3
EPS = 1e-6

F32 = jnp.float32
BF16 = jnp.bfloat16
I32 = jnp.int32
INT_MIN = -(2 ** 31)

SUBLANES = 8
MXU_DIM = 256

TOKEN_TILE = 256
Q_TILE = MXU_DIM
K_TILE = 256
SCORE_ROWS = 64
FFN_CHUNK = 256
FFN_LOOKAHEAD = 2
VMEM_LIMIT = 56 * 1024 * 1024

_NT = (((1,), (1,)), ((), ()))


def _rms(x, g):
    return x * lax.rsqrt(jnp.mean(x * x, axis=-1, keepdims=True) + EPS) * g


def _shifted_rows(buf_ref, u, first_tile):
    tm = u.shape[0]

    @pl.when(first_tile)
    def _():
        buf_ref[0:SUBLANES, :] = jnp.zeros((SUBLANES, u.shape[1]), F32)

    buf_ref[SUBLANES:SUBLANES + tm, :] = u
    u1 = buf_ref[SUBLANES - 1:SUBLANES - 1 + tm, :]
    u2 = buf_ref[SUBLANES - 2:SUBLANES - 2 + tm, :]
    buf_ref[0:SUBLANES, :] = u[tm - SUBLANES:, :]
    return u1, u2


def _proj_kernel(x_ref, g_ref, wk_ref, wik_ref, wconv_ref, wg_ref,
                 wqT_ref, wvT_ref, wiqT_ref, wiwT_ref,
                 bg_ref, cw_ref, wco_ref,
                 k_ref, ik_ref, qT_ref, vT_ref, iqT_ref, iwT_ref, ga_ref, mc_ref,
                 ubuf_ref):
    j = pl.program_id(1)
    xn = _rms(x_ref[0], g_ref[...]).astype(BF16)

    pc = jnp.dot(xn, wconv_ref[...], preferred_element_type=F32)
    ch = pc[:, 0:CONV_WIDTH]
    cb = pc[:, CONV_WIDTH:2 * CONV_WIDTH]
    cc = pc[:, 2 * CONV_WIDTH:3 * CONV_WIDTH]
    u = cc * ch
    u1, u2 = _shifted_rows(ubuf_ref, u, j == 0)
    cw = cw_ref[...]
    conv = u2 * cw[0:1, :] + u1 * cw[1:2, :] + u * cw[2:3, :]
    cbconv = (cb * conv).astype(BF16)

    g = jax.nn.sigmoid(jnp.dot(xn, wg_ref[...], preferred_element_type=F32)
                       + bg_ref[...])
    ga_ref[0] = g[:, 0:D_MODEL]

    qT = lax.dot_general(wqT_ref[...], xn, _NT, preferred_element_type=F32)
    qT_ref[0] = (qT * (HEAD_DIM ** -0.5)).astype(BF16)
    vT = lax.dot_general(wvT_ref[...], xn, _NT, preferred_element_type=F32)
    vT_ref[0] = vT.astype(BF16)
    iqT_ref[0] = lax.dot_general(wiqT_ref[...], xn, _NT, preferred_element_type=F32)
    iwT = lax.dot_general(wiwT_ref[...], xn, _NT, preferred_element_type=F32)
    iwT_ref[0] = iwT[0:IDX_HEADS, :]

    kk = jnp.dot(xn, wk_ref[...], preferred_element_type=F32)
    for h in range(N_HEADS):
        k_ref[0, h] = kk[:, h * HEAD_DIM:(h + 1) * HEAD_DIM].astype(BF16)
    ik = jnp.dot(xn, wik_ref[...], preferred_element_type=F32)
    ik_ref[0] = ik[:, 0:IDX_DIM]

    bc = jnp.dot(cbconv, wco_ref[...], preferred_element_type=F32)
    mc_ref[0] = g[:, D_MODEL:2 * D_MODEL] * bc


def _attn_kernel(iqT_ref, iwT_ref, qT_ref, ik_ref, k_ref, vT_ref, o_ref,
                 keys_ref, bias_ref, l_ref, oT_ref, tlim_ref, *, seq_len, topk):
    i = pl.program_id(1)
    nch = i + 1
    row_k = lax.broadcasted_iota(I32, (K_TILE, Q_TILE), 0)
    lane_q = lax.broadcasted_iota(I32, (K_TILE, Q_TILE), 1)
    rel = row_k - lane_q

    def chunk_start(c):
        return pl.multiple_of(c * K_TILE, K_TILE)

    def fold_rows(m):
        return jnp.sum(m.reshape(K_TILE // SUBLANES, SUBLANES, Q_TILE), axis=0)

    iw = iwT_ref[0] * ((IDX_HEADS ** -0.5) * (IDX_DIM ** -0.5))

    def score_chunk(c, carry):
        for sb in range(K_TILE // SCORE_ROWS):
            r0 = pl.multiple_of(c * K_TILE + sb * SCORE_ROWS, SCORE_ROWS)
            ikc = ik_ref[0, pl.ds(r0, SCORE_ROWS), :].astype(BF16)
            acc = jnp.zeros((SCORE_ROWS, Q_TILE), F32)
            for h in range(IDX_HEADS):
                iq_h = iqT_ref[0, h * IDX_DIM:(h + 1) * IDX_DIM, :].astype(BF16)
                d = jnp.dot(ikc, iq_h, preferred_element_type=F32)
                acc = acc + jnp.maximum(d, 0.0) * iw[h:h + 1, :]
            acc = jnp.where(acc == 0.0, 0.0, acc)
            bits = pltpu.bitcast(acc, I32)
            key = bits ^ ((bits >> 31) & 0x7FFFFFFF)
            valid = (rel[0:SCORE_ROWS, :] + sb * SCORE_ROWS) <= (i - c) * K_TILE
            keys_ref[pl.ds(r0, SCORE_ROWS), :] = jnp.where(valid, key, INT_MIN)
        return carry

    lax.fori_loop(0, nch, score_chunk, 0)

    def count_where(pred):
        def body(c, acc):
            r0 = chunk_start(c)
            kc = keys_ref[pl.ds(r0, K_TILE), :]
            return acc + fold_rows(jnp.where(pred(kc, r0), 1, 0).astype(I32))
        acc = lax.fori_loop(0, nch, body, jnp.zeros((SUBLANES, Q_TILE), I32))
        return jnp.sum(acc, axis=0, keepdims=True)

    def radix_step(t, thr):
        bit = lax.shift_left(jnp.int32(1), 31 - t)
        cand = thr ^ bit
        cnt = count_where(lambda kc, r0: kc >= cand)
        return jnp.where(cnt >= topk, cand, thr)

    thr = lax.fori_loop(0, 32, radix_step, jnp.full((1, Q_TILE), INT_MIN, I32))
    thr = jnp.maximum(thr, INT_MIN + 1)
    cnt_gt = count_where(lambda kc, r0: kc > thr)
    cnt_eq = count_where(lambda kc, r0: kc == thr)
    need = topk - cnt_gt

    tlim_ref[...] = jnp.full((SUBLANES, Q_TILE), seq_len, I32)

    @pl.when(jnp.max(cnt_eq - need) > 0)
    def _():
        def tie_step(t, lim):
            bit = lax.shift_left(jnp.int32(1), (seq_len.bit_length() - 2) - t)
            cand = lim | bit
            cnt = count_where(
                lambda kc, r0: jnp.logical_and(kc == thr, (row_k + r0) < cand))
            return jnp.where(cnt < need, cand, lim)
        lim = lax.fori_loop(0, seq_len.bit_length() - 1, tie_step,
                            jnp.zeros((1, Q_TILE), I32))
        tlim_ref[...] = jnp.broadcast_to(lim, (SUBLANES, Q_TILE))

    tlim = tlim_ref[0:1, :]

    def logits_chunk(c, m8):
        r0 = chunk_start(c)
        kc = keys_ref[pl.ds(r0, K_TILE), :]
        tie_ok = jnp.logical_and(kc == thr, (row_k + r0) <= tlim)
        sel = jnp.logical_or(kc > thr, tie_ok)
        bias_ref[...] = jnp.where(sel, 0.0, -jnp.inf).astype(F32)
        out = []
        for h in range(N_HEADS):
            hs = slice(h * HEAD_DIM, (h + 1) * HEAD_DIM)
            l = jnp.dot(k_ref[0, h, pl.ds(r0, K_TILE), :], qT_ref[0, hs, :],
                        preferred_element_type=F32) + bias_ref[...]
            l_ref[h, pl.ds(r0, K_TILE), :] = l
            out.append(jnp.maximum(
                m8[h], jnp.max(l.reshape(K_TILE // SUBLANES, SUBLANES, Q_TILE), axis=0)))
        return tuple(out)

    m8 = lax.fori_loop(
        0, nch, logits_chunk,
        tuple(jnp.full((SUBLANES, Q_TILE), -jnp.inf, F32) for _ in range(N_HEADS)))
    m_all = [jnp.max(m, axis=0, keepdims=True) for m in m8]

    oT_ref[...] = jnp.zeros(oT_ref.shape, F32)

    def pv_chunk(c, s8):
        r0 = chunk_start(c)
        out = []
        for h in range(N_HEADS):
            hs = slice(h * HEAD_DIM, (h + 1) * HEAD_DIM)
            p = jnp.exp(l_ref[h, pl.ds(r0, K_TILE), :] - m_all[h])
            out.append(s8[h] + fold_rows(p))
            oT_ref[hs, :] += jnp.dot(vT_ref[0, hs, pl.ds(r0, K_TILE)], p.astype(BF16),
                                     preferred_element_type=F32)
        return tuple(out)

    s8 = lax.fori_loop(
        0, nch, pv_chunk,
        tuple(jnp.zeros((SUBLANES, Q_TILE), F32) for _ in range(N_HEADS)))
    for h in range(N_HEADS):
        hs = slice(h * HEAD_DIM, (h + 1) * HEAD_DIM)
        oT_ref[hs, :] = oT_ref[hs, :] / jnp.sum(s8[h], axis=0, keepdims=True)

    o_ref[0] = oT_ref[...].T.astype(BF16)


def _ffn_kernel(x_ref, attn_ref, ga_ref, mc_ref, wao_ref, wo_ref, g2_ref,
                wup_ref, fcw_ref, wdn_ref, gf_ref, out_ref, gbuf_ref, vbuf_ref):
    j = pl.program_id(1)
    tm = x_ref.shape[1]
    ba = jnp.dot(attn_ref[0], wao_ref[...], preferred_element_type=F32)
    merged = (ga_ref[0] * ba + mc_ref[0]).astype(BF16)
    h1 = x_ref[0] + jnp.dot(merged, wo_ref[...], preferred_element_type=F32)
    xn2 = _rms(h1, g2_ref[...]).astype(BF16)

    @pl.when(j == 0)
    def _():
        zeros = jnp.zeros((SUBLANES, FFN_CHUNK), F32)
        for n in range(D_FF // FFN_CHUNK):
            gbuf_ref[n, 0:SUBLANES, :] = zeros
            vbuf_ref[n, 0:SUBLANES, :] = zeros

    def causal_conv(u, buf_ref, col0):
        buf_ref[SUBLANES:SUBLANES + tm, :] = u
        u1 = buf_ref[SUBLANES - 1:SUBLANES - 1 + tm, :]
        u2 = buf_ref[SUBLANES - 2:SUBLANES - 2 + tm, :]
        buf_ref[0:SUBLANES, :] = u[tm - SUBLANES:, :]
        w = fcw_ref[:, col0:col0 + FFN_CHUNK]
        return u2 * w[0:1, :] + u1 * w[1:2, :] + u * w[2:3, :]

    def up_proj(n):
        c0 = n * FFN_CHUNK
        ug = jnp.dot(xn2, wup_ref[:, c0:c0 + FFN_CHUNK], preferred_element_type=F32)
        uv = jnp.dot(xn2, wup_ref[:, D_FF + c0:D_FF + c0 + FFN_CHUNK],
                     preferred_element_type=F32)
        return ug, uv

    nchunk = D_FF // FFN_CHUNK
    acc = h1
    ahead = [up_proj(n) for n in range(min(FFN_LOOKAHEAD, nchunk))]
    for n in range(nchunk):
        c0 = n * FFN_CHUNK
        ug, uv = ahead.pop(0)
        if n + FFN_LOOKAHEAD < nchunk:
            ahead.append(up_proj(n + FFN_LOOKAHEAD))
        gt = causal_conv(ug, gbuf_ref.at[n], c0)
        val = causal_conv(uv, vbuf_ref.at[n], D_FF + c0)
        act = (gt * jax.nn.sigmoid(gt) * val).astype(BF16)
        acc = acc + jnp.dot(act, wdn_ref[c0:c0 + FFN_CHUNK, :], preferred_element_type=F32)

    out_ref[0] = _rms(acc, gf_ref[...])


def _const_spec(shape):
    nd = len(shape)
    return pl.BlockSpec(shape, lambda b, j: (0,) * nd, pipeline_mode=pl.Buffered(1))


def kernel(x, norm_mix_g, w_in, b_gate, w_attn_out, conv_w, w_conv_out, w_o,
           norm_ffn_g, w_up, ffn_conv_w, w_down, norm_final_g):
    B, S, D = x.shape
    assert D == D_MODEL and S % TOKEN_TILE == 0 and S % Q_TILE == 0 and Q_TILE == K_TILE
    assert norm_mix_g.shape[0] == 1, "single-layer block"
    topk = min(INDEX_TOPK_MAX, S // 4)
    tm = TOKEN_TILE
    nt = S // tm

    w = w_in[0]
    o_q, o_k, o_v, o_iq = 0, ATTN_WIDTH, 2 * ATTN_WIDTH, 3 * ATTN_WIDTH
    o_ik = o_iq + IDX_HEADS * IDX_DIM
    o_iw = o_ik + IDX_DIM
    o_ch = o_iw + IDX_HEADS
    o_g = o_ch + 3 * CONV_WIDTH
    wqT = w[:, o_q:o_k].T.astype(BF16)
    wvT = w[:, o_v:o_iq].T.astype(BF16)
    wiqT = w[:, o_iq:o_ik].T.astype(BF16)
    wiwT = jnp.pad(w[:, o_iw:o_ch].T, ((0, 16 - IDX_HEADS), (0, 0))).astype(BF16)
    wk = w[:, o_k:o_v].astype(BF16)
    wik = jnp.pad(w[:, o_ik:o_iw], ((0, 0), (0, 128 - IDX_DIM))).astype(BF16)
    wconv = w[:, o_ch:o_g].astype(BF16)
    wg = w[:, o_g:].astype(BF16)
    g1 = norm_mix_g.reshape(1, D)
    bg = b_gate.reshape(1, 2 * D)
    cw = conv_w[0]
    wco = w_conv_out[0].astype(BF16)

    cparams = pltpu.CompilerParams(
        dimension_semantics=("arbitrary", "arbitrary"), vmem_limit_bytes=VMEM_LIMIT)

    tok = lambda c: pl.BlockSpec((1, tm, c), lambda b, j: (b, j, 0))
    featT = lambda c, t: pl.BlockSpec((1, c, t), lambda b, j: (b, 0, j))

    k, ik, qT, vT, iqT, iwT, ga, mc = pl.pallas_call(
        _proj_kernel,
        grid=(B, nt),
        in_specs=[tok(D), _const_spec((1, D)),
                  _const_spec(wk.shape), _const_spec(wik.shape),
                  _const_spec(wconv.shape), _const_spec(wg.shape),
                  _const_spec(wqT.shape), _const_spec(wvT.shape),
                  _const_spec(wiqT.shape), _const_spec(wiwT.shape),
                  _const_spec(bg.shape), _const_spec(cw.shape), _const_spec(wco.shape)],
        out_specs=[pl.BlockSpec((1, N_HEADS, tm, HEAD_DIM), lambda b, j: (b, 0, j, 0)),
                   tok(IDX_DIM), featT(ATTN_WIDTH, tm), featT(ATTN_WIDTH, tm),
                   featT(IDX_HEADS * IDX_DIM, tm), featT(IDX_HEADS, tm),
                   tok(D), tok(D)],
        out_shape=[jax.ShapeDtypeStruct((B, N_HEADS, S, HEAD_DIM), BF16),
                   jax.ShapeDtypeStruct((B, S, IDX_DIM), F32),
                   jax.ShapeDtypeStruct((B, ATTN_WIDTH, S), BF16),
                   jax.ShapeDtypeStruct((B, ATTN_WIDTH, S), BF16),
                   jax.ShapeDtypeStruct((B, IDX_HEADS * IDX_DIM, S), F32),
                   jax.ShapeDtypeStruct((B, IDX_HEADS, S), F32),
                   jax.ShapeDtypeStruct((B, S, D), F32),
                   jax.ShapeDtypeStruct((B, S, D), F32)],
        scratch_shapes=[pltpu.VMEM((tm + SUBLANES, CONV_WIDTH), F32)],
        compiler_params=cparams,
        name="proj",
    )(x, g1, wk, wik, wconv, wg, wqT, wvT, wiqT, wiwT, bg, cw, wco)

    nq = S // Q_TILE
    attn = pl.pallas_call(
        functools.partial(_attn_kernel, seq_len=S, topk=topk),
        grid=(B, nq),
        in_specs=[featT(IDX_HEADS * IDX_DIM, Q_TILE), featT(IDX_HEADS, Q_TILE),
                  featT(ATTN_WIDTH, Q_TILE),
                  pl.BlockSpec((1, S, IDX_DIM), lambda b, j: (b, 0, 0)),
                  pl.BlockSpec((1, N_HEADS, S, HEAD_DIM), lambda b, j: (b, 0, 0, 0)),
                  pl.BlockSpec((1, ATTN_WIDTH, S), lambda b, j: (b, 0, 0))],
        out_specs=pl.BlockSpec((1, Q_TILE, ATTN_WIDTH), lambda b, j: (b, j, 0)),
        out_shape=jax.ShapeDtypeStruct((B, S, ATTN_WIDTH), BF16),
        scratch_shapes=[pltpu.VMEM((S, Q_TILE), I32),
                        pltpu.VMEM((K_TILE, Q_TILE), F32),
                        pltpu.VMEM((N_HEADS, S, Q_TILE), F32),
                        pltpu.VMEM((ATTN_WIDTH, Q_TILE), F32),
                        pltpu.VMEM((SUBLANES, Q_TILE), I32)],
        compiler_params=cparams,
        name="attn",
    )(iqT, iwT, qT, ik, k, vT)

    wao = w_attn_out[0].astype(BF16)
    wo = w_o[0].astype(BF16)
    wup = w_up[0].astype(BF16)
    wdn = w_down[0].astype(BF16)
    fcw = ffn_conv_w[0]
    g2 = norm_ffn_g.reshape(1, D)
    gf = norm_final_g.reshape(1, D)
    nchunk = D_FF // FFN_CHUNK
    assert nchunk * FFN_CHUNK == D_FF

    out = pl.pallas_call(
        _ffn_kernel,
        grid=(B, nt),
        in_specs=[tok(D), tok(ATTN_WIDTH), tok(D), tok(D),
                  _const_spec(wao.shape), _const_spec(wo.shape), _const_spec(g2.shape),
                  _const_spec(wup.shape), _const_spec(fcw.shape), _const_spec(wdn.shape),
                  _const_spec(gf.shape)],
        out_specs=tok(D),
        out_shape=jax.ShapeDtypeStruct((B, S, D), F32),
        scratch_shapes=[pltpu.VMEM((nchunk, tm + SUBLANES, FFN_CHUNK), F32),
                        pltpu.VMEM((nchunk, tm + SUBLANES, FFN_CHUNK), F32)],
        compiler_params=cparams,
        name="ffn",
    )(x, attn, ga, mc, wao, wo, g2, wup, fcw, wdn, gf)
    return out
```

```python
import functools

import jax
import jax.numpy as jnp
from jax import lax
from jax.experimental import pallas as pl
from jax.experimental.pallas import tpu as pltpu

D_MODEL = 1024
N_HEADS = 8
HEAD_DIM = 64
ATTN_WIDTH = N_HEADS * HEAD_DIM
IDX_HEADS = 8
IDX_DIM = 64
INDEX_TOPK_MAX = 256
CONV_WIDTH = 512
CONV_K = 3
D_FF = 2816
FFN_CONV_K = 3
EPS = 1e-6
LOG2E = 1.4426950408889634

F32 = jnp.float32
BF16 = jnp.bfloat16
I32 = jnp.int32
I16 = jnp.int16
INT_MIN = -(2 ** 31)

SUBLANES = 8
PACK_ROWS = 16
HALF_BIAS = 1 << 15
MXU_DIM = 256

TOKEN_TILE = 256
Q_TILE = MXU_DIM
K_TILE = 256
SCORE_ROWS = 64
FFN_CHUNK = 256
FFN_LOOKAHEAD = 2
VMEM_LIMIT = 56 * 1024 * 1024

_NT = (((1,), (1,)), ((), ()))


def _rms(x, g):
    return x * lax.rsqrt(jnp.mean(x * x, axis=-1, keepdims=True) + EPS) * g


def _shifted_rows(buf_ref, u, first_tile):
    tm = u.shape[0]

    @pl.when(first_tile)
    def _():
        buf_ref[0:SUBLANES, :] = jnp.zeros((SUBLANES, u.shape[1]), F32)

    buf_ref[SUBLANES:SUBLANES + tm, :] = u
    u1 = buf_ref[SUBLANES - 1:SUBLANES - 1 + tm, :]
    u2 = buf_ref[SUBLANES - 2:SUBLANES - 2 + tm, :]
    buf_ref[0:SUBLANES, :] = u[tm - SUBLANES:, :]
    return u1, u2


def _proj_kernel(x_ref, g_ref, wk_ref, wik_ref, wconv_ref, wg_ref,
                 wqT_ref, wvT_ref, wiqT_ref, wiwT_ref,
                 bg_ref, cw_ref, wco_ref,
                 k_ref, ik_ref, qT_ref, vT_ref, iqT_ref, iwT_ref, ga_ref, mc_ref,
                 ubuf_ref):
    j = pl.program_id(1)
    xn = _rms(x_ref[0], g_ref[...]).astype(BF16)

    pc = jnp.dot(xn, wconv_ref[...], preferred_element_type=F32)
    ch = pc[:, 0:CONV_WIDTH]
    cb = pc[:, CONV_WIDTH:2 * CONV_WIDTH]
    cc = pc[:, 2 * CONV_WIDTH:3 * CONV_WIDTH]
    u = cc * ch
    u1, u2 = _shifted_rows(ubuf_ref, u, j == 0)
    cw = cw_ref[...]
    conv = u2 * cw[0:1, :] + u1 * cw[1:2, :] + u * cw[2:3, :]
    cbconv = (cb * conv).astype(BF16)

    g = jax.nn.sigmoid(jnp.dot(xn, wg_ref[...], preferred_element_type=F32)
                       + bg_ref[...])
    ga_ref[0] = g[:, 0:D_MODEL]

    qT = lax.dot_general(wqT_ref[...], xn, _NT, preferred_element_type=F32)
    qT_ref[0] = (qT * (HEAD_DIM ** -0.5 * LOG2E)).astype(BF16)
    vT = lax.dot_general(wvT_ref[...], xn, _NT, preferred_element_type=F32)
    vT_ref[0] = vT.astype(BF16)
    iqT = lax.dot_general(wiqT_ref[...], xn, _NT, preferred_element_type=F32)
    iqT_ref[0] = iqT.astype(BF16)
    iwT = lax.dot_general(wiwT_ref[...], xn, _NT, preferred_element_type=F32)
    iwT_ref[0] = iwT[0:IDX_HEADS, :]

    kk = jnp.dot(xn, wk_ref[...], preferred_element_type=F32)
    for h in range(N_HEADS):
        k_ref[0, h] = kk[:, h * HEAD_DIM:(h + 1) * HEAD_DIM].astype(BF16)
    ik = jnp.dot(xn, wik_ref[...], preferred_element_type=F32)
    ik_ref[0] = ik[:, 0:IDX_DIM].astype(BF16)

    bc = jnp.dot(cbconv, wco_ref[...], preferred_element_type=F32)
    mc_ref[0] = g[:, D_MODEL:2 * D_MODEL] * bc


def _attn_kernel(iqT_ref, iwT_ref, qT_ref, ik_ref, k_ref, vT_ref, o_ref,
                 keys_ref, half_ref, bias_ref, l_ref, oT_ref, tlim_ref, *, seq_len, topk):
    i = pl.program_id(1)
    nch = i + 1
    row_k = lax.broadcasted_iota(I32, (K_TILE, Q_TILE), 0)
    lane_q = lax.broadcasted_iota(I32, (K_TILE, Q_TILE), 1)
    rel = row_k - lane_q

    def chunk_start(c):
        return pl.multiple_of(c * K_TILE, K_TILE)

    def fold_rows(m):
        return jnp.sum(m.reshape(K_TILE // SUBLANES, SUBLANES, Q_TILE), axis=0)

    iw = iwT_ref[0] * ((IDX_HEADS ** -0.5) * (IDX_DIM ** -0.5))

    def score_chunk(c, carry):
        for sb in range(K_TILE // SCORE_ROWS):
            r0 = pl.multiple_of(c * K_TILE + sb * SCORE_ROWS, SCORE_ROWS)
            ikc = ik_ref[0, pl.ds(r0, SCORE_ROWS), :]
            acc = jnp.zeros((SCORE_ROWS, Q_TILE), F32)
            for h in range(IDX_HEADS):
                iq_h = iqT_ref[0, h * IDX_DIM:(h + 1) * IDX_DIM, :]
                d = jnp.dot(ikc, iq_h, preferred_element_type=F32)
                acc = acc + jnp.maximum(d, 0.0) * iw[h:h + 1, :]
            acc = jnp.where(acc == 0.0, 0.0, acc)
            bits = pltpu.bitcast(acc, I32)
            key = bits ^ ((bits >> 31) & 0x7FFFFFFF)
            valid = (rel[0:SCORE_ROWS, :] + sb * SCORE_ROWS) <= (i - c) * K_TILE
            key = jnp.where(valid, key, INT_MIN)
            keys_ref[pl.ds(r0, SCORE_ROWS), :] = key
            g0 = c * (K_TILE // PACK_ROWS) + sb * (SCORE_ROWS // PACK_ROWS)
            half_ref[pl.ds(g0, SCORE_ROWS // PACK_ROWS)] = (
                (key >> 16).astype(I16).reshape(SCORE_ROWS // PACK_ROWS, PACK_ROWS, Q_TILE))
        return carry

    lax.fori_loop(0, nch, score_chunk, 0)

    def count_where(pred):
        def body(c, acc):
            r0 = chunk_start(c)
            kc = keys_ref[pl.ds(r0, K_TILE), :]
            return acc + fold_rows(jnp.where(pred(kc, r0), 1, 0).astype(I32))
        acc = lax.fori_loop(0, nch, body, jnp.zeros((SUBLANES, Q_TILE), I32))
        return jnp.sum(acc, axis=0, keepdims=True)

    def select16(kth):
        groups = K_TILE // PACK_ROWS

        def step(t, thr):
            bit = lax.shift_left(jnp.int32(1), 15 - t)
            cand = ((thr + HALF_BIAS) ^ bit) - HALF_BIAS
            cand16 = jnp.broadcast_to(cand, (PACK_ROWS, Q_TILE)).astype(I16)

            def body(c, acc):
                g0 = pl.multiple_of(c * groups, groups)
                hc = half_ref[pl.ds(g0, groups)]
                one = jnp.where(hc >= cand16[None], jnp.int16(1), jnp.int16(0))
                for g in range(groups):
                    acc = acc + one[g]
                return acc

            acc = lax.fori_loop(0, nch, body, jnp.zeros((PACK_ROWS, Q_TILE), I16))
            cnt = jnp.sum(acc.astype(I32), axis=0, keepdims=True)
            return jnp.where(cnt >= kth, cand, thr)

        return lax.fori_loop(0, 16, step, jnp.full((1, Q_TILE), -HALF_BIAS, I32))

    thr_hi = select16(topk)
    cnt_hi_gt = count_where(lambda kc, r0: (kc >> 16) > thr_hi)

    def low_chunk(c, carry):
        r0 = chunk_start(c)
        kc = keys_ref[pl.ds(r0, K_TILE), :]
        lo = jnp.where((kc >> 16) == thr_hi, (kc & 0xFFFF) - HALF_BIAS, -HALF_BIAS)
        half_ref[pl.ds(c * (K_TILE // PACK_ROWS), K_TILE // PACK_ROWS)] = (
            lo.astype(I16).reshape(K_TILE // PACK_ROWS, PACK_ROWS, Q_TILE))
        return carry

    lax.fori_loop(0, nch, low_chunk, 0)
    thr_lo = select16(topk - cnt_hi_gt)
    thr = (thr_hi << 16) | (thr_lo + HALF_BIAS)
    thr = jnp.maximum(thr, INT_MIN + 1)
    cnt_gt = count_where(lambda kc, r0: kc > thr)
    cnt_eq = count_where(lambda kc, r0: kc == thr)
    need = topk - cnt_gt

    tlim_ref[...] = jnp.full((SUBLANES, Q_TILE), seq_len, I32)

    @pl.when(jnp.max(cnt_eq - need) > 0)
    def _():
        def tie_step(t, lim):
            bit = lax.shift_left(jnp.int32(1), (seq_len.bit_length() - 2) - t)
            cand = lim | bit
            cnt = count_where(
                lambda kc, r0: jnp.logical_and(kc == thr, (row_k + r0) < cand))
            return jnp.where(cnt < need, cand, lim)
        lim = lax.fori_loop(0, seq_len.bit_length() - 1, tie_step,
                            jnp.zeros((1, Q_TILE), I32))
        tlim_ref[...] = jnp.broadcast_to(lim, (SUBLANES, Q_TILE))

    tlim = tlim_ref[0:1, :]

    def logits_chunk(c, m8):
        r0 = chunk_start(c)
        kc = keys_ref[pl.ds(r0, K_TILE), :]
        tie_ok = jnp.logical_and(kc == thr, (row_k + r0) <= tlim)
        sel = jnp.logical_or(kc > thr, tie_ok)
        bias_ref[...] = jnp.where(sel, 0.0, -jnp.inf).astype(F32)
        out = []
        for h in range(N_HEADS):
            hs = slice(h * HEAD_DIM, (h + 1) * HEAD_DIM)
            l = jnp.dot(k_ref[0, h, pl.ds(r0, K_TILE), :], qT_ref[0, hs, :],
                        preferred_element_type=F32) + bias_ref[...]
            l_ref[h, pl.ds(r0, K_TILE), :] = l
            out.append(jnp.maximum(
                m8[h], jnp.max(l.reshape(K_TILE // SUBLANES, SUBLANES, Q_TILE), axis=0)))
        return tuple(out)

    m8 = lax.fori_loop(
        0, nch, logits_chunk,
        tuple(jnp.full((SUBLANES, Q_TILE), -jnp.inf, F32) for _ in range(N_HEADS)))
    m_all = [jnp.max(m, axis=0, keepdims=True) for m in m8]

    oT_ref[...] = jnp.zeros(oT_ref.shape, F32)

    def pv_chunk(c, s8):
        r0 = chunk_start(c)
        out = []
        for h in range(N_HEADS):
            hs = slice(h * HEAD_DIM, (h + 1) * HEAD_DIM)
            p = jnp.exp2(l_ref[h, pl.ds(r0, K_TILE), :] - m_all[h])
            out.append(s8[h] + fold_rows(p))
            oT_ref[hs, :] += jnp.dot(vT_ref[0, hs, pl.ds(r0, K_TILE)], p.astype(BF16),
                                     preferred_element_type=F32)
        return tuple(out)

    s8 = lax.fori_loop(
        0, nch, pv_chunk,
        tuple(jnp.zeros((SUBLANES, Q_TILE), F32) for _ in range(N_HEADS)))
    for h in range(N_HEADS):
        hs = slice(h * HEAD_DIM, (h + 1) * HEAD_DIM)
        oT_ref[hs, :] = oT_ref[hs, :] / jnp.sum(s8[h], axis=0, keepdims=True)

    o_ref[0] = oT_ref[...].T.astype(BF16)


def _ffn_kernel(x_ref, attn_ref, ga_ref, mc_ref, wao_ref, wo_ref, g2_ref,
                wup_ref, fcw_ref, wdn_ref, gf_ref, out_ref, gbuf_ref, vbuf_ref):
    j = pl.program_id(1)
    tm = x_ref.shape[1]
    ba = jnp.dot(attn_ref[0], wao_ref[...], preferred_element_type=F32)
    merged = (ga_ref[0] * ba + mc_ref[0]).astype(BF16)
    h1 = x_ref[0] + jnp.dot(merged, wo_ref[...], preferred_element_type=F32)
    xn2 = _rms(h1, g2_ref[...]).astype(BF16)

    @pl.when(j == 0)
    def _():
        zeros = jnp.zeros((SUBLANES, FFN_CHUNK), F32)
        for n in range(D_FF // FFN_CHUNK):
            gbuf_ref[n, 0:SUBLANES, :] = zeros
            vbuf_ref[n, 0:SUBLANES, :] = zeros

    def causal_conv(u, buf_ref, col0):
        buf_ref[SUBLANES:SUBLANES + tm, :] = u
        u1 = buf_ref[SUBLANES - 1:SUBLANES - 1 + tm, :]
        u2 = buf_ref[SUBLANES - 2:SUBLANES - 2 + tm, :]
        buf_ref[0:SUBLANES, :] = u[tm - SUBLANES:, :]
        w = fcw_ref[:, col0:col0 + FFN_CHUNK]
        return u2 * w[0:1, :] + u1 * w[1:2, :] + u * w[2:3, :]

    def up_proj(n):
        c0 = n * FFN_CHUNK
        ug = jnp.dot(xn2, wup_ref[:, c0:c0 + FFN_CHUNK], preferred_element_type=F32)
        uv = jnp.dot(xn2, wup_ref[:, D_FF + c0:D_FF + c0 + FFN_CHUNK],
                     preferred_element_type=F32)
        return ug, uv

    nchunk = D_FF // FFN_CHUNK
    acc = h1
    ahead = [up_proj(n) for n in range(min(FFN_LOOKAHEAD, nchunk))]
    for n in range(nchunk):
        c0 = n * FFN_CHUNK
        ug, uv = ahead.pop(0)
        if n + FFN_LOOKAHEAD < nchunk:
            ahead.append(up_proj(n + FFN_LOOKAHEAD))
        gt = causal_conv(ug, gbuf_ref.at[n], c0)
        val = causal_conv(uv, vbuf_ref.at[n], D_FF + c0)
        act = (gt * jax.nn.sigmoid(gt) * val).astype(BF16)
        acc = acc + jnp.dot(act, wdn_ref[c0:c0 + FFN_CHUNK, :], preferred_element_type=F32)

    out_ref[0] = _rms(acc, gf_ref[...])


def _const_spec(shape):
    nd = len(shape)
    return pl.BlockSpec(shape, lambda b, j: (0,) * nd, pipeline_mode=pl.Buffered(1))


def kernel(x, norm_mix_g, w_in, b_gate, w_attn_out, conv_w, w_conv_out, w_o,
           norm_ffn_g, w_up, ffn_conv_w, w_down, norm_final_g):
    B, S, D = x.shape
    assert D == D_MODEL and S % TOKEN_TILE == 0 and S % Q_TILE == 0 and Q_TILE == K_TILE
    assert norm_mix_g.shape[0] == 1, "single-layer block"
    topk = min(INDEX_TOPK_MAX, S // 4)
    tm = TOKEN_TILE
    nt = S // tm

    w = w_in[0]
    o_q, o_k, o_v, o_iq = 0, ATTN_WIDTH, 2 * ATTN_WIDTH, 3 * ATTN_WIDTH
    o_ik = o_iq + IDX_HEADS * IDX_DIM
    o_iw = o_ik + IDX_DIM
    o_ch = o_iw + IDX_HEADS
    o_g = o_ch + 3 * CONV_WIDTH
    wqT = w[:, o_q:o_k].T.astype(BF16)
    wvT = w[:, o_v:o_iq].T.astype(BF16)
    wiqT = w[:, o_iq:o_ik].T.astype(BF16)
    wiwT = jnp.pad(w[:, o_iw:o_ch].T, ((0, 16 - IDX_HEADS), (0, 0))).astype(BF16)
    wk = w[:, o_k:o_v].astype(BF16)
    wik = jnp.pad(w[:, o_ik:o_iw], ((0, 0), (0, 128 - IDX_DIM))).astype(BF16)
    wconv = w[:, o_ch:o_g].astype(BF16)
    wg = w[:, o_g:].astype(BF16)
    g1 = norm_mix_g.reshape(1, D)
    bg = b_gate.reshape(1, 2 * D)
    cw = conv_w[0]
    wco = w_conv_out[0].astype(BF16)

    cparams = pltpu.CompilerParams(
        dimension_semantics=("arbitrary", "arbitrary"), vmem_limit_bytes=VMEM_LIMIT)

    tok = lambda c: pl.BlockSpec((1, tm, c), lambda b, j: (b, j, 0))
    featT = lambda c, t: pl.BlockSpec((1, c, t), lambda b, j: (b, 0, j))

    k, ik, qT, vT, iqT, iwT, ga, mc = pl.pallas_call(
        _proj_kernel,
        grid=(B, nt),
        in_specs=[tok(D), _const_spec((1, D)),
                  _const_spec(wk.shape), _const_spec(wik.shape),
                  _const_spec(wconv.shape), _const_spec(wg.shape),
                  _const_spec(wqT.shape), _const_spec(wvT.shape),
                  _const_spec(wiqT.shape), _const_spec(wiwT.shape),
                  _const_spec(bg.shape), _const_spec(cw.shape), _const_spec(wco.shape)],
        out_specs=[pl.BlockSpec((1, N_HEADS, tm, HEAD_DIM), lambda b, j: (b, 0, j, 0)),
                   tok(IDX_DIM), featT(ATTN_WIDTH, tm), featT(ATTN_WIDTH, tm),
                   featT(IDX_HEADS * IDX_DIM, tm), featT(IDX_HEADS, tm),
                   tok(D), tok(D)],
        out_shape=[jax.ShapeDtypeStruct((B, N_HEADS, S, HEAD_DIM), BF16),
                   jax.ShapeDtypeStruct((B, S, IDX_DIM), BF16),
                   jax.ShapeDtypeStruct((B, ATTN_WIDTH, S), BF16),
                   jax.ShapeDtypeStruct((B, ATTN_WIDTH, S), BF16),
                   jax.ShapeDtypeStruct((B, IDX_HEADS * IDX_DIM, S), BF16),
                   jax.ShapeDtypeStruct((B, IDX_HEADS, S), F32),
                   jax.ShapeDtypeStruct((B, S, D), F32),
                   jax.ShapeDtypeStruct((B, S, D), F32)],
        scratch_shapes=[pltpu.VMEM((tm + SUBLANES, CONV_WIDTH), F32)],
        compiler_params=cparams,
        name="proj",
    )(x, g1, wk, wik, wconv, wg, wqT, wvT, wiqT, wiwT, bg, cw, wco)

    nq = S // Q_TILE
    attn = pl.pallas_call(
        functools.partial(_attn_kernel, seq_len=S, topk=topk),
        grid=(B, nq),
        in_specs=[featT(IDX_HEADS * IDX_DIM, Q_TILE), featT(IDX_HEADS, Q_TILE),
                  featT(ATTN_WIDTH, Q_TILE),
                  pl.BlockSpec((1, S, IDX_DIM), lambda b, j: (b, 0, 0)),
                  pl.BlockSpec((1, N_HEADS, S, HEAD_DIM), lambda b, j: (b, 0, 0, 0)),
                  pl.BlockSpec((1, ATTN_WIDTH, S), lambda b, j: (b, 0, 0))],
        out_specs=pl.BlockSpec((1, Q_TILE, ATTN_WIDTH), lambda b, j: (b, j, 0)),
        out_shape=jax.ShapeDtypeStruct((B, S, ATTN_WIDTH), BF16),
        scratch_shapes=[pltpu.VMEM((S, Q_TILE), I32),
                        pltpu.VMEM((S // PACK_ROWS, PACK_ROWS, Q_TILE), I16),
                        pltpu.VMEM((K_TILE, Q_TILE), F32),
                        pltpu.VMEM((N_HEADS, S, Q_TILE), F32),
                        pltpu.VMEM((ATTN_WIDTH, Q_TILE), F32),
                        pltpu.VMEM((SUBLANES, Q_TILE), I32)],
        compiler_params=cparams,
        name="attn",
    )(iqT, iwT, qT, ik, k, vT)

    wao = w_attn_out[0].astype(BF16)
    wo = w_o[0].astype(BF16)
    wup = w_up[0].astype(BF16)
    wdn = w_down[0].astype(BF16)
    fcw = ffn_conv_w[0]
    g2 = norm_ffn_g.reshape(1, D)
    gf = norm_final_g.reshape(1, D)
    nchunk = D_FF // FFN_CHUNK
    assert nchunk * FFN_CHUNK == D_FF

    out = pl.pallas_call(
        _ffn_kernel,
        grid=(B, nt),
        in_specs=[tok(D), tok(ATTN_WIDTH), tok(D), tok(D),
                  _const_spec(wao.shape), _const_spec(wo.shape), _const_spec(g2.shape),
                  _const_spec(wup.shape), _const_spec(fcw.shape), _const_spec(wdn.shape),
                  _const_spec(gf.shape)],
        out_specs=tok(D),
        out_shape=jax.ShapeDtypeStruct((B, S, D), F32),
        scratch_shapes=[pltpu.VMEM((nchunk, tm + SUBLANES, FFN_CHUNK), F32),
                        pltpu.VMEM((nchunk, tm + SUBLANES, FFN_CHUNK), F32)],
        compiler_params=cparams,
        name="ffn",
    )(x, attn, ga, mc, wao, wo, g2, wup, fcw, wdn, gf)
    return out
```

```python
import functools

import jax
import jax.numpy as jnp
from jax import lax
from jax.experimental import pallas as pl
from jax.experimental.pallas import tpu as pltpu

D_MODEL = 1024
N_HEADS = 8
HEAD_DIM = 64
ATTN_WIDTH = N_HEADS * HEAD_DIM
IDX_HEADS = 8
IDX_DIM = 64
INDEX_TOPK_MAX = 256
CONV_WIDTH = 512
D_FF = 2816
EPS = 1e-6
LOG2E = 1.4426950408889634

F32 = jnp.float32
BF16 = jnp.bfloat16
I32 = jnp.int32
KEY_BITS = 32
INT_MIN = -(2 ** 31)

SUBLANES = 8
BF16_ROWS = 16
LANES = 128
MXU_DIM = 256
VMEM_LIMIT = 56 * 1024 * 1024

PROJ_TILE = 512
FFN_TILE = 256
Q_TILE = MXU_DIM
K_TILE = 256
SCORE_ROWS = 64
EARLY_QK_HEADS = 4
FFN_CHUNK = 256
FFN_LOOKAHEAD = 2
DOWN_GROUP = 4
MERGE_AT_CHUNK = 8
NORM_AT_CHUNK = 10

_NT = (((1,), (1,)), ((), ()))


def _rms(x, g):
    return x * lax.rsqrt(jnp.mean(x * x, axis=-1, keepdims=True) + EPS) * g


def _shifted_rows(buf_ref, u, first_tile):
    tm = u.shape[0]

    @pl.when(first_tile)
    def _():
        buf_ref[0:SUBLANES, :] = jnp.zeros((SUBLANES, u.shape[1]), F32)

    buf_ref[SUBLANES:SUBLANES + tm, :] = u
    u1 = buf_ref[SUBLANES - 1:SUBLANES - 1 + tm, :]
    u2 = buf_ref[SUBLANES - 2:SUBLANES - 2 + tm, :]
    buf_ref[0:SUBLANES, :] = u[tm - SUBLANES:, :]
    return u1, u2


def _proj_kernel(x_ref, g_ref, wk_ref, wik_ref, wconv_ref, wg_ref,
                 wqT_ref, wvT_ref, wiqT_ref, wiwT_ref,
                 bg_ref, cw_ref, wco_ref,
                 k_ref, ik_ref, qT_ref, vT_ref, iqT_ref, iwT_ref, ga_ref, mc_ref,
                 ubuf_ref):
    j = pl.program_id(1)
    xn = _rms(x_ref[0], g_ref[...]).astype(BF16)

    pc = jnp.dot(xn, wconv_ref[...], preferred_element_type=F32)
    ch = pc[:, 0:CONV_WIDTH]
    cb = pc[:, CONV_WIDTH:2 * CONV_WIDTH]
    cc = pc[:, 2 * CONV_WIDTH:3 * CONV_WIDTH]
    u = cc * ch
    u1, u2 = _shifted_rows(ubuf_ref, u, j == 0)
    cw = cw_ref[...]
    conv = u2 * cw[0:1, :] + u1 * cw[1:2, :] + u * cw[2:3, :]
    cbconv = (cb * conv).astype(BF16)

    g = jax.nn.sigmoid(jnp.dot(xn, wg_ref[...], preferred_element_type=F32)
                       + bg_ref[...])
    ga_ref[0] = g[:, 0:D_MODEL]

    qT = lax.dot_general(wqT_ref[...], xn, _NT, preferred_element_type=F32)
    qT_ref[0] = (qT * (HEAD_DIM ** -0.5 * LOG2E)).astype(BF16)
    vT = lax.dot_general(wvT_ref[...], xn, _NT, preferred_element_type=F32)
    vT_ref[0] = vT.astype(BF16)
    iqT = lax.dot_general(wiqT_ref[...], xn, _NT, preferred_element_type=F32)
    iqT_ref[0] = iqT.astype(BF16)
    iwT = lax.dot_general(wiwT_ref[...], xn, _NT, preferred_element_type=F32)
    iwT_ref[0] = iwT[0:IDX_HEADS, :]

    kk = jnp.dot(xn, wk_ref[...], preferred_element_type=F32)
    for h in range(N_HEADS):
        k_ref[0, h] = kk[:, h * HEAD_DIM:(h + 1) * HEAD_DIM].astype(BF16)
    ik = jnp.dot(xn, wik_ref[...], preferred_element_type=F32)
    ik_ref[0] = ik[:, 0:IDX_DIM].astype(BF16)

    bc = jnp.dot(cbconv, wco_ref[...], preferred_element_type=F32)
    mc_ref[0] = g[:, D_MODEL:2 * D_MODEL] * bc


def _bit_transpose32(a):
    rows = [a[j] for j in range(KEY_BITS)]
    j, m = KEY_BITS // 2, 0x0000FFFF
    while j:
        for k in range(KEY_BITS):
            if k & j:
                continue
            t = (rows[k] ^ lax.shift_right_logical(rows[k + j], jnp.int32(j))) & m
            rows[k] = rows[k] ^ t
            rows[k + j] = rows[k + j] ^ lax.shift_left(t, jnp.int32(j))
        j >>= 1
        m = (m ^ (m << j)) & 0xFFFFFFFF
    return jnp.stack(rows)


def _attn_kernel(iqT_ref, iwT_ref, qT_ref, ik_ref, k_ref, vT_ref, *rest, seq_len, topk, n_cast):
    cast_in, (o_ref, *rest) = rest[:n_cast], rest[n_cast:]
    cast_out, scratch = rest[:n_cast], rest[n_cast:]
    keys_ref, planes_ref, bias_ref, l_ref, oT_ref, tlim_ref = scratch
    for src, dst in zip(cast_in, cast_out):
        dst[...] = src[...].astype(BF16)
    assert K_TILE == KEY_BITS * SUBLANES
    max_chunks = seq_len // K_TILE
    i = pl.program_id(1)
    nch = i + 1
    row_k = lax.broadcasted_iota(I32, (K_TILE, Q_TILE), 0)
    lane_q = lax.broadcasted_iota(I32, (K_TILE, Q_TILE), 1)
    rel = row_k - lane_q

    def chunk_start(c):
        return pl.multiple_of(c * K_TILE, K_TILE)

    def chunk_loop(body, init):
        def pair(p, carry):
            return body(2 * p + 1, 1, body(2 * p, 0, carry))
        carry = lax.fori_loop(0, lax.shift_right_logical(nch, 1), pair, init)
        return lax.cond((nch & 1) == 1, lambda cr: body(nch - 1, 0, cr), lambda cr: cr, carry)

    def qk_logits(h, r0):
        return jnp.dot(k_ref[0, h, pl.ds(r0, K_TILE), :],
                       qT_ref[0, h * HEAD_DIM:(h + 1) * HEAD_DIM, :],
                       preferred_element_type=F32)

    def fold_rows(m):
        return jnp.sum(m.reshape(K_TILE // SUBLANES, SUBLANES, Q_TILE), axis=0)

    iw = iwT_ref[0] * ((IDX_HEADS ** -0.5) * (IDX_DIM ** -0.5))

    def score_chunk(c, slot, carry):
        for sb in range(K_TILE // SCORE_ROWS):
            r0 = pl.multiple_of(c * K_TILE + sb * SCORE_ROWS, SCORE_ROWS)
            ikc = ik_ref[0, pl.ds(r0, SCORE_ROWS), :]
            acc = jnp.zeros((SCORE_ROWS, Q_TILE), F32)
            for h in range(IDX_HEADS):
                iq_h = iqT_ref[0, h * IDX_DIM:(h + 1) * IDX_DIM, :]
                d = jnp.dot(ikc, iq_h, preferred_element_type=F32)
                acc = acc + jnp.maximum(d, 0.0) * iw[h:h + 1, :]
            acc = jnp.where(acc == 0.0, 0.0, acc)
            valid = (rel[0:SCORE_ROWS, :] + sb * SCORE_ROWS) <= (i - c) * K_TILE
            keys_ref[pl.ds(r0, SCORE_ROWS), :] = jnp.where(valid, acc, -jnp.inf)
        for h in range(EARLY_QK_HEADS):
            l_ref[h, pl.ds(chunk_start(c), K_TILE), :] = qk_logits(h, chunk_start(c))
        return carry

    chunk_loop(score_chunk, 0)

    def count_where(pred):
        def body(c, slot, acc):
            r0 = chunk_start(c)
            sc = keys_ref[pl.ds(r0, K_TILE), :]
            return acc + fold_rows(jnp.where(pred(sc, r0), 1, 0).astype(I32))
        acc = chunk_loop(body, jnp.zeros((SUBLANES, Q_TILE), I32))
        return jnp.sum(acc, axis=0, keepdims=True)

    def key_to_float(key):
        return pltpu.bitcast(key ^ ((key >> 31) & 0x7FFFFFFF), F32)

    def radix_step(t, thr_key):
        cand = thr_key ^ lax.shift_left(jnp.int32(1), KEY_BITS - 1 - t)
        cand_f = key_to_float(cand)
        cnt = count_where(lambda sc, r0: sc >= cand_f)
        return jnp.where(cnt >= topk, cand, thr_key)

    thr_key = lax.fori_loop(0, KEY_BITS, radix_step, jnp.full((1, Q_TILE), INT_MIN, I32))
    thr = key_to_float(thr_key)
    has_k = thr > -jnp.inf
    thr = jnp.where(has_k, thr, -jnp.inf)
    n_gt = count_where(lambda sc, r0: sc > thr)
    n_eq = count_where(lambda sc, r0: sc == thr)
    need = topk - n_gt

    tlim_ref[...] = jnp.full((SUBLANES, Q_TILE), seq_len, I32)

    @pl.when(jnp.max(jnp.where(has_k, n_eq - need, 0)) > 0)
    def _():
        def tie_step(t, lim):
            bit = lax.shift_left(jnp.int32(1), (seq_len.bit_length() - 2) - t)
            cand = lim | bit
            cnt = count_where(
                lambda sc, r0: jnp.logical_and(sc == thr, (row_k + r0) < cand))
            return jnp.where(cnt < need, cand, lim)
        lim = lax.fori_loop(0, seq_len.bit_length() - 1, tie_step,
                            jnp.zeros((1, Q_TILE), I32))
        tlim_ref[...] = jnp.broadcast_to(lim, (SUBLANES, Q_TILE))

    tlim = tlim_ref[0:1, :]

    def logits_chunk(c, slot, m8):
        r0 = chunk_start(c)
        kc = keys_ref[pl.ds(r0, K_TILE), :]
        tie_ok = jnp.logical_and(kc == thr, (row_k + r0) <= tlim)
        sel = jnp.logical_and(rel <= (i - c) * K_TILE, jnp.logical_or(kc > thr, tie_ok))
        bias_ref[slot] = jnp.where(sel, 0.0, -jnp.inf).astype(F32)
        out = []
        for h in range(N_HEADS):
            raw = l_ref[h, pl.ds(r0, K_TILE), :] if h < EARLY_QK_HEADS else qk_logits(h, r0)
            l = raw + bias_ref[slot]
            l_ref[h, pl.ds(r0, K_TILE), :] = l
            out.append(jnp.maximum(
                m8[h], jnp.max(l.reshape(K_TILE // SUBLANES, SUBLANES, Q_TILE), axis=0)))
        return tuple(out)

    m8 = chunk_loop(
        logits_chunk,
        tuple(jnp.full((SUBLANES, Q_TILE), -jnp.inf, F32) for _ in range(N_HEADS)))
    m_all = [jnp.max(m, axis=0, keepdims=True) for m in m8]

    oT_ref[...] = jnp.zeros(oT_ref.shape, F32)

    def pv_chunk(c, slot, s8):
        r0 = chunk_start(c)
        out = []
        for h in range(N_HEADS):
            hs = slice(h * HEAD_DIM, (h + 1) * HEAD_DIM)
            p = jnp.exp2(l_ref[h, pl.ds(r0, K_TILE), :] - m_all[h])
            out.append(s8[h] + fold_rows(p))
            oT_ref[hs, :] += jnp.dot(vT_ref[0, hs, pl.ds(r0, K_TILE)], p.astype(BF16),
                                     preferred_element_type=F32)
        return tuple(out)

    s8 = chunk_loop(
        pv_chunk, tuple(jnp.zeros((SUBLANES, Q_TILE), F32) for _ in range(N_HEADS)))
    for h in range(N_HEADS):
        hs = slice(h * HEAD_DIM, (h + 1) * HEAD_DIM)
        oT_ref[hs, :] = oT_ref[hs, :] / jnp.sum(s8[h], axis=0, keepdims=True)

    o_ref[0] = oT_ref[...].T.astype(BF16)


def _ffn_kernel(x_ref, attn_ref, ga_ref, mc_ref, wao_ref, wo_ref, g2_ref,
                wup_ref, fcw_ref, wdn_ref, gf_ref, out_ref,
                gbuf_ref, vbuf_ref, h1_ref, xn2_ref, merged_ref, *, tiles_per_seq):
    t = pl.program_id(0)
    tm = x_ref.shape[1]
    cur = lax.rem(t + 1, 2)
    nxt = lax.rem(t, 2)

    @pl.when(t == 0)
    def _():
        h1_ref[1] = jnp.zeros(h1_ref.shape[1:], F32)
        xn2_ref[1] = jnp.zeros(xn2_ref.shape[1:], BF16)

    @pl.when(jnp.logical_or(t == 0, lax.rem(t - 1, tiles_per_seq) == 0))
    def _():
        zeros = jnp.zeros((SUBLANES, FFN_CHUNK), F32)
        for n in range(D_FF // FFN_CHUNK):
            gbuf_ref[n, 0:SUBLANES, :] = zeros
            vbuf_ref[n, 0:SUBLANES, :] = zeros

    def merge_next():
        ba = jnp.dot(attn_ref[0], wao_ref[...], preferred_element_type=F32)
        return (ga_ref[0] * ba + mc_ref[0]).astype(BF16)

    def norm_next(merged):
        h1n = x_ref[0] + jnp.dot(merged, wo_ref[...], preferred_element_type=F32)
        h1_ref[nxt] = h1n
        xn2_ref[nxt] = _rms(h1n, g2_ref[...]).astype(BF16)

    xn2 = xn2_ref[cur]

    def causal_conv(u, buf_ref, col0):
        buf_ref[SUBLANES:SUBLANES + tm, :] = u
        u1 = buf_ref[SUBLANES - 1:SUBLANES - 1 + tm, :]
        u2 = buf_ref[SUBLANES - 2:SUBLANES - 2 + tm, :]
        buf_ref[0:SUBLANES, :] = u[tm - SUBLANES:, :]
        w = fcw_ref[:, col0:col0 + FFN_CHUNK]
        return u2 * w[0:1, :] + u1 * w[1:2, :] + u * w[2:3, :]

    def up_proj(n):
        c0 = n * FFN_CHUNK
        ug = jnp.dot(xn2, wup_ref[:, c0:c0 + FFN_CHUNK], preferred_element_type=F32)
        uv = jnp.dot(xn2, wup_ref[:, D_FF + c0:D_FF + c0 + FFN_CHUNK],
                     preferred_element_type=F32)
        return ug, uv

    nchunk = D_FF // FFN_CHUNK
    out_ref[0] = h1_ref[cur]
    ahead = [up_proj(n) for n in range(min(FFN_LOOKAHEAD, nchunk))]
    acts = []
    for n in range(nchunk):
        c0 = n * FFN_CHUNK
        ug, uv = ahead.pop(0)
        if n + FFN_LOOKAHEAD < nchunk:
            ahead.append(up_proj(n + FFN_LOOKAHEAD))
        if n == MERGE_AT_CHUNK:
            merged_ref[...] = merge_next()
        if n == NORM_AT_CHUNK:
            norm_next(merged_ref[...])
        gt = causal_conv(ug, gbuf_ref.at[n], c0)
        val = causal_conv(uv, vbuf_ref.at[n], D_FF + c0)
        acts.append((gt * jax.nn.sigmoid(gt) * val).astype(BF16))
        if len(acts) == DOWN_GROUP or n == nchunk - 1:
            r0 = (n + 1 - len(acts)) * FFN_CHUNK
            act = acts[0] if len(acts) == 1 else jnp.concatenate(acts, axis=1)
            out_ref[0] += jnp.dot(act, wdn_ref[r0:r0 + act.shape[1], :],
                                  preferred_element_type=F32)
            acts = []

    if NORM_AT_CHUNK >= nchunk:
        norm_next(merged_ref[...])
    out_ref[0] = _rms(out_ref[0], gf_ref[...])


def _sparse_attention(iqT, iwT, qT, ik, k, vT, *, topk, cast_along=()):
    B, _, S = qT.shape
    nq = S // Q_TILE
    featT = lambda c: pl.BlockSpec((1, c, Q_TILE), lambda b, j: (b, 0, j))

    def rows_spec(w):
        rows = next(r for r in range(BF16_ROWS, w.shape[0] + 1, BF16_ROWS)
                    if w.shape[0] % r == 0 and w.shape[0] // r <= B * nq)
        last = w.shape[0] // rows - 1
        return pl.BlockSpec((rows, w.shape[1]), lambda b, j: (jnp.minimum(b * nq + j, last), 0))

    cast_specs = [rows_spec(w) for w in cast_along]
    outs = pl.pallas_call(
        functools.partial(_attn_kernel, seq_len=S, topk=topk, n_cast=len(cast_along)),
        grid=(B, nq),
        in_specs=[featT(IDX_HEADS * IDX_DIM), featT(IDX_HEADS), featT(ATTN_WIDTH),
                  pl.BlockSpec((1, S, IDX_DIM), lambda b, j: (b, 0, 0)),
                  pl.BlockSpec((1, N_HEADS, S, HEAD_DIM), lambda b, j: (b, 0, 0, 0)),
                  pl.BlockSpec((1, ATTN_WIDTH, S), lambda b, j: (b, 0, 0))] + cast_specs,
        out_specs=[pl.BlockSpec((1, Q_TILE, ATTN_WIDTH), lambda b, j: (b, j, 0))] + cast_specs,
        out_shape=[jax.ShapeDtypeStruct((B, S, ATTN_WIDTH), BF16)]
        + [jax.ShapeDtypeStruct(w.shape, BF16) for w in cast_along],
        scratch_shapes=[pltpu.VMEM((S, Q_TILE), F32),
                        pltpu.VMEM((S // K_TILE, KEY_BITS, SUBLANES, Q_TILE), I32),
                        pltpu.VMEM((2, K_TILE, Q_TILE), F32),
                        pltpu.VMEM((N_HEADS, S, Q_TILE), F32),
                        pltpu.VMEM((ATTN_WIDTH, Q_TILE), F32),
                        pltpu.VMEM((SUBLANES, Q_TILE), I32)],
        compiler_params=pltpu.CompilerParams(
            dimension_semantics=("arbitrary", "arbitrary"), vmem_limit_bytes=VMEM_LIMIT),
        name="attn",
    )(iqT, iwT, qT, ik, k, vT, *cast_along)
    return outs[0], tuple(outs[1:])


def _const_spec(shape):
    nd = len(shape)
    return pl.BlockSpec(shape, lambda *_: (0,) * nd, pipeline_mode=pl.Buffered(1))


def kernel(x, norm_mix_g, w_in, b_gate, w_attn_out, conv_w, w_conv_out, w_o,
           norm_ffn_g, w_up, ffn_conv_w, w_down, norm_final_g):
    B, S, D = x.shape
    assert D == D_MODEL and S % PROJ_TILE == 0 and S % FFN_TILE == 0
    assert S % Q_TILE == 0 and Q_TILE == K_TILE
    assert norm_mix_g.shape[0] == 1, "single-layer block"
    topk = min(INDEX_TOPK_MAX, S // 4)
    tm = PROJ_TILE

    w = w_in[0]
    o_q, o_k, o_v, o_iq = 0, ATTN_WIDTH, 2 * ATTN_WIDTH, 3 * ATTN_WIDTH
    o_ik = o_iq + IDX_HEADS * IDX_DIM
    o_iw = o_ik + IDX_DIM
    o_ch = o_iw + IDX_HEADS
    o_g = o_ch + 3 * CONV_WIDTH
    wqT = w[:, o_q:o_k].T.astype(BF16)
    wvT = w[:, o_v:o_iq].T.astype(BF16)
    wiqT = w[:, o_iq:o_ik].T.astype(BF16)
    wiwT = jnp.pad(w[:, o_iw:o_ch].T, ((0, BF16_ROWS - IDX_HEADS), (0, 0))).astype(BF16)
    wk = w[:, o_k:o_v].astype(BF16)
    wik = jnp.pad(w[:, o_ik:o_iw], ((0, 0), (0, LANES - IDX_DIM))).astype(BF16)
    wconv = w[:, o_ch:o_g].astype(BF16)
    wg = w[:, o_g:].astype(BF16)
    g1 = norm_mix_g.reshape(1, D)
    bg = b_gate.reshape(1, 2 * D)
    cw = conv_w[0]
    wco = w_conv_out[0].astype(BF16)

    cparams = pltpu.CompilerParams(
        dimension_semantics=("arbitrary", "arbitrary"), vmem_limit_bytes=VMEM_LIMIT)

    tok = lambda c, t: pl.BlockSpec((1, t, c), lambda b, j: (b, j, 0))
    featT = lambda c, t: pl.BlockSpec((1, c, t), lambda b, j: (b, 0, j))

    k, ik, qT, vT, iqT, iwT, ga, mc = pl.pallas_call(
        _proj_kernel,
        grid=(B, S // tm),
        in_specs=[tok(D, tm), _const_spec((1, D)),
                  _const_spec(wk.shape), _const_spec(wik.shape),
                  _const_spec(wconv.shape), _const_spec(wg.shape),
                  _const_spec(wqT.shape), _const_spec(wvT.shape),
                  _const_spec(wiqT.shape), _const_spec(wiwT.shape),
                  _const_spec(bg.shape), _const_spec(cw.shape), _const_spec(wco.shape)],
        out_specs=[pl.BlockSpec((1, N_HEADS, tm, HEAD_DIM), lambda b, j: (b, 0, j, 0)),
                   tok(IDX_DIM, tm), featT(ATTN_WIDTH, tm), featT(ATTN_WIDTH, tm),
                   featT(IDX_HEADS * IDX_DIM, tm), featT(IDX_HEADS, tm),
                   tok(D, tm), tok(D, tm)],
        out_shape=[jax.ShapeDtypeStruct((B, N_HEADS, S, HEAD_DIM), BF16),
                   jax.ShapeDtypeStruct((B, S, IDX_DIM), BF16),
                   jax.ShapeDtypeStruct((B, ATTN_WIDTH, S), BF16),
                   jax.ShapeDtypeStruct((B, ATTN_WIDTH, S), BF16),
                   jax.ShapeDtypeStruct((B, IDX_HEADS * IDX_DIM, S), BF16),
                   jax.ShapeDtypeStruct((B, IDX_HEADS, S), F32),
                   jax.ShapeDtypeStruct((B, S, D), F32),
                   jax.ShapeDtypeStruct((B, S, D), F32)],
        scratch_shapes=[pltpu.VMEM((tm + SUBLANES, CONV_WIDTH), F32)],
        compiler_params=cparams,
        name="proj",
    )(x, g1, wk, wik, wconv, wg, wqT, wvT, wiqT, wiwT, bg, cw, wco)

    attn, (wao, wo, wup, wdn) = _sparse_attention(
        iqT, iwT, qT, ik, k, vT, topk=topk,
        cast_along=(w_attn_out[0], w_o[0], w_up[0], w_down[0]))
    fcw = ffn_conv_w[0]
    g2 = norm_ffn_g.reshape(1, D)
    gf = norm_final_g.reshape(1, D)
    nchunk = D_FF // FFN_CHUNK
    assert nchunk * FFN_CHUNK == D_FF
    tf = FFN_TILE

    nt = S // tf
    n_tiles = B * nt

    def tile_in(c):
        def index(t):
            tt = jnp.minimum(t, n_tiles - 1)
            return (tt // nt, tt % nt, 0)
        return pl.BlockSpec((1, tf, c), index)

    def tile_out(t):
        tt = jnp.maximum(t - 1, 0)
        return (tt // nt, tt % nt, 0)

    out = pl.pallas_call(
        functools.partial(_ffn_kernel, tiles_per_seq=nt),
        grid=(n_tiles + 1,),
        in_specs=[tile_in(D), tile_in(ATTN_WIDTH), tile_in(D), tile_in(D),
                  _const_spec(wao.shape), _const_spec(wo.shape), _const_spec(g2.shape),
                  _const_spec(wup.shape), _const_spec(fcw.shape), _const_spec(wdn.shape),
                  _const_spec(gf.shape)],
        out_specs=pl.BlockSpec((1, tf, D), tile_out),
        out_shape=jax.ShapeDtypeStruct((B, S, D), F32),
        scratch_shapes=[pltpu.VMEM((nchunk, tf + SUBLANES, FFN_CHUNK), F32),
                        pltpu.VMEM((nchunk, tf + SUBLANES, FFN_CHUNK), F32),
                        pltpu.VMEM((2, tf, D), F32),
                        pltpu.VMEM((2, tf, D), BF16),
                        pltpu.VMEM((tf, D), BF16)],
        compiler_params=pltpu.CompilerParams(
            dimension_semantics=("arbitrary",), vmem_limit_bytes=VMEM_LIMIT),
        name="ffn",
    )(x, attn, ga, mc, wao, wo, g2, wup, fcw, wdn, gf)
    return out
```

```python
import functools

import jax
import jax.numpy as jnp
from jax import lax
from jax.experimental import pallas as pl
from jax.experimental.pallas import tpu as pltpu

D_MODEL = 1024
N_HEADS = 8
HEAD_DIM = 64
ATTN_WIDTH = N_HEADS * HEAD_DIM
IDX_HEADS = 8
IDX_DIM = 64
INDEX_TOPK_MAX = 256
CONV_WIDTH = 512
D_FF = 2816
EPS = 1e-6
LOG2E = 1.4426950408889634

F32 = jnp.float32
BF16 = jnp.bfloat16
I32 = jnp.int32
KEY_BITS = 32
INT_MIN = -(2 ** 31)

SUBLANES = 8
BF16_ROWS = 16
LANES = 128
MXU_DIM = 256
VMEM_LIMIT = 56 * 1024 * 1024

PROJ_TILE = 512
FFN_TILE = 256
Q_TILE = MXU_DIM
K_TILE = 256
SCORE_ROWS = 64
EARLY_QK_HEADS = 4
FFN_CHUNK = 256
FFN_LOOKAHEAD = 2
DOWN_GROUP = 4
MERGE_AT_CHUNK = 8
NORM_AT_CHUNK = 10

_NT = (((1,), (1,)), ((), ()))


def _rms(x, g):
    return x * lax.rsqrt(jnp.mean(x * x, axis=-1, keepdims=True) + EPS) * g


def _shifted_rows(buf_ref, u, first_tile):
    tm = u.shape[0]

    @pl.when(first_tile)
    def _():
        buf_ref[0:SUBLANES, :] = jnp.zeros((SUBLANES, u.shape[1]), F32)

    buf_ref[SUBLANES:SUBLANES + tm, :] = u
    u1 = buf_ref[SUBLANES - 1:SUBLANES - 1 + tm, :]
    u2 = buf_ref[SUBLANES - 2:SUBLANES - 2 + tm, :]
    buf_ref[0:SUBLANES, :] = u[tm - SUBLANES:, :]
    return u1, u2


def _proj_kernel(x_ref, g_ref, wk_ref, wik_ref, wconv_ref, wg_ref,
                 wqT_ref, wvT_ref, wiqT_ref, wiwT_ref,
                 bg_ref, cw_ref, wco_ref,
                 k_ref, ik_ref, qT_ref, vT_ref, iqT_ref, iwT_ref, ga_ref, mc_ref,
                 ubuf_ref):
    j = pl.program_id(1)
    xn = _rms(x_ref[0], g_ref[...]).astype(BF16)

    pc = jnp.dot(xn, wconv_ref[...], preferred_element_type=F32)
    ch = pc[:, 0:CONV_WIDTH]
    cb = pc[:, CONV_WIDTH:2 * CONV_WIDTH]
    cc = pc[:, 2 * CONV_WIDTH:3 * CONV_WIDTH]
    u = cc * ch
    u1, u2 = _shifted_rows(ubuf_ref, u, j == 0)
    cw = cw_ref[...]
    conv = u2 * cw[0:1, :] + u1 * cw[1:2, :] + u * cw[2:3, :]
    cbconv = (cb * conv).astype(BF16)

    g = jax.nn.sigmoid(jnp.dot(xn, wg_ref[...], preferred_element_type=F32)
                       + bg_ref[...])
    ga_ref[0] = g[:, 0:D_MODEL]

    qT = lax.dot_general(wqT_ref[...], xn, _NT, preferred_element_type=F32)
    qT_ref[0] = (qT * (HEAD_DIM ** -0.5 * LOG2E)).astype(BF16)
    vT = lax.dot_general(wvT_ref[...], xn, _NT, preferred_element_type=F32)
    vT_ref[0] = vT.astype(BF16)
    iqT = lax.dot_general(wiqT_ref[...], xn, _NT, preferred_element_type=F32)
    iqT_ref[0] = iqT.astype(BF16)
    iwT = lax.dot_general(wiwT_ref[...], xn, _NT, preferred_element_type=F32)
    iwT_ref[0] = iwT[0:IDX_HEADS, :]

    kk = jnp.dot(xn, wk_ref[...], preferred_element_type=F32)
    for h in range(N_HEADS):
        k_ref[0, h] = kk[:, h * HEAD_DIM:(h + 1) * HEAD_DIM].astype(BF16)
    ik = jnp.dot(xn, wik_ref[...], preferred_element_type=F32)
    ik_ref[0] = ik[:, 0:IDX_DIM].astype(BF16)

    bc = jnp.dot(cbconv, wco_ref[...], preferred_element_type=F32)
    mc_ref[0] = g[:, D_MODEL:2 * D_MODEL] * bc


def _bit_transpose32(a):
    rows = [a[j] for j in range(KEY_BITS)]
    j, m = KEY_BITS // 2, 0x0000FFFF
    while j:
        for k in range(KEY_BITS):
            if k & j:
                continue
            t = (rows[k] ^ lax.shift_right_logical(rows[k + j], jnp.int32(j))) & m
            rows[k] = rows[k] ^ t
            rows[k + j] = rows[k + j] ^ lax.shift_left(t, jnp.int32(j))
        j >>= 1
        m = (m ^ (m << j)) & 0xFFFFFFFF
    return jnp.stack(rows)


def _attn_kernel(iqT_ref, iwT_ref, qT_ref, ik_ref, k_ref, vT_ref, *rest, seq_len, topk, n_cast):
    cast_in, (o_ref, *rest) = rest[:n_cast], rest[n_cast:]
    cast_out, scratch = rest[:n_cast], rest[n_cast:]
    sc_ref, planes_ref, bias_ref, l_ref, oT_ref, tlim_ref, ties_ref = scratch
    for src, dst in zip(cast_in, cast_out):
        dst[...] = src[...].astype(BF16)
    assert K_TILE == KEY_BITS * SUBLANES
    max_chunks = seq_len // K_TILE
    i = pl.program_id(1)
    nch = i + 1
    row_k = lax.broadcasted_iota(I32, (K_TILE, Q_TILE), 0)
    lane_q = lax.broadcasted_iota(I32, (K_TILE, Q_TILE), 1)
    rel = row_k - lane_q

    def chunk_start(c):
        return pl.multiple_of(c * K_TILE, K_TILE)

    def chunk_loop(body, init):
        def pair(p, carry):
            return body(2 * p + 1, 1, body(2 * p, 0, carry))
        carry = lax.fori_loop(0, lax.shift_right_logical(nch, 1), pair, init)
        return lax.cond((nch & 1) == 1, lambda cr: body(nch - 1, 0, cr), lambda cr: cr, carry)

    def qk_logits(h, r0):
        return jnp.dot(k_ref[0, h, pl.ds(r0, K_TILE), :],
                       qT_ref[0, h * HEAD_DIM:(h + 1) * HEAD_DIM, :],
                       preferred_element_type=F32)

    def fold_rows(m):
        return jnp.sum(m.reshape(K_TILE // SUBLANES, SUBLANES, Q_TILE), axis=0)

    iw = iwT_ref[0] * ((IDX_HEADS ** -0.5) * (IDX_DIM ** -0.5))

    def score_chunk(c, slot, carry):
        for sb in range(K_TILE // SCORE_ROWS):
            r0 = pl.multiple_of(c * K_TILE + sb * SCORE_ROWS, SCORE_ROWS)
            ikc = ik_ref[0, pl.ds(r0, SCORE_ROWS), :]
            acc = jnp.zeros((SCORE_ROWS, Q_TILE), F32)
            for h in range(IDX_HEADS):
                iq_h = iqT_ref[0, h * IDX_DIM:(h + 1) * IDX_DIM, :]
                d = jnp.dot(ikc, iq_h, preferred_element_type=F32)
                acc = acc + jnp.maximum(d, 0.0) * iw[h:h + 1, :]
            acc = jnp.where(acc == 0.0, 0.0, acc)
            valid = (rel[0:SCORE_ROWS, :] + sb * SCORE_ROWS) <= (i - c) * K_TILE
            sc_ref[pl.ds(r0, SCORE_ROWS), :] = jnp.where(valid, acc, -jnp.inf)
        for h in range(EARLY_QK_HEADS):
            l_ref[h, pl.ds(chunk_start(c), K_TILE), :] = qk_logits(h, chunk_start(c))
        for lt in range(Q_TILE // LANES):
            ls = slice(lt * LANES, (lt + 1) * LANES)
            bits = pltpu.bitcast(sc_ref[pl.ds(chunk_start(c), K_TILE), ls], I32)
            u = bits ^ ((bits >> 31) | INT_MIN)
            planes_ref[c, :, :, ls] = _bit_transpose32(u.reshape(KEY_BITS, SUBLANES, LANES))
        return carry

    chunk_loop(score_chunk, 0)
    for c in range(1, max_chunks):
        @pl.when(c >= nch)
        def _(c=c):
            planes_ref[c] = jnp.zeros(planes_ref.shape[1:], I32)

    def count(words):
        pops = [lax.population_count(w) for w in words]
        while len(pops) > 1:
            pops = [a + b for a, b in zip(pops[0::2], pops[1::2])]
        return jnp.sum(pops[0], axis=0, keepdims=True)

    def radix_step(r, carry):
        match, n_gt, thr_u = carry
        ones = [match[c] & planes_ref[c, r] for c in range(max_chunks)]
        n_one = count(ones)
        take = (n_gt + n_one) >= topk
        thr_u = jnp.where(take, thr_u | lax.shift_left(jnp.int32(1), KEY_BITS - 1 - r), thr_u)
        n_gt = jnp.where(take, n_gt, n_gt + n_one)
        match = tuple(jnp.where(take, ones[c], match[c] ^ ones[c]) for c in range(max_chunks))
        return match, n_gt, thr_u

    match0 = tuple(jnp.broadcast_to(jnp.where(c < nch, -1, 0), (SUBLANES, Q_TILE)).astype(I32)
                   for c in range(max_chunks))
    match, n_gt, thr_u = lax.fori_loop(
        0, KEY_BITS, radix_step,
        (match0, jnp.zeros((1, Q_TILE), I32), jnp.zeros((1, Q_TILE), I32)))
    key = thr_u ^ INT_MIN
    thr = pltpu.bitcast(key ^ ((key >> 31) & 0x7FFFFFFF), F32)
    thr = jnp.where(thr > -jnp.inf, thr, -jnp.inf)

    def rank_counts(t):
        def body(c, slot, carry):
            sc = sc_ref[pl.ds(chunk_start(c), K_TILE), :]
            return (carry[0] + fold_rows(jnp.where(sc > t, 1, 0).astype(I32)),
                    carry[1] + fold_rows(jnp.where(sc >= t, 1, 0).astype(I32)))
        zero = jnp.zeros((SUBLANES, Q_TILE), I32)
        gt8, ge8 = chunk_loop(body, (zero, zero))
        return jnp.sum(gt8, axis=0, keepdims=True), jnp.sum(ge8, axis=0, keepdims=True)

    def off_by(state):
        _, gt, ge, steps = state
        bad = jnp.logical_or(gt >= topk, ge < topk)
        return jnp.logical_and(jnp.max(jnp.where(bad, 1, 0)) > 0, steps < topk)

    def walk(state):
        t, gt, ge, steps = state

        lower = lambda a, b: jnp.where(a < b, a, b)
        upper = lambda a, b: jnp.where(a > b, a, b)

        def fold(op, x):
            while x.shape[0] > 1:
                half = x.shape[0] // 2
                x = op(x[:half], x[half:])
            return x

        def body(c, slot, carry):
            sc = sc_ref[pl.ds(chunk_start(c), K_TILE), :]
            return (lower(carry[0], fold(lower, jnp.where(sc > t, sc, jnp.inf))),
                    upper(carry[1], fold(upper, jnp.where(sc < t, sc, -jnp.inf))))
        up, dn = chunk_loop(body, (jnp.full((1, Q_TILE), jnp.inf, F32),
                                   jnp.full((1, Q_TILE), -jnp.inf, F32)))
        t = jnp.where(gt >= topk, up, jnp.where(ge < topk, dn, t))
        return (t,) + rank_counts(t) + (steps + 1,)

    thr, n_gt, n_ge, _ = lax.while_loop(off_by, walk, (thr,) + rank_counts(thr) + (jnp.int32(0),))
    n_eq = n_ge - n_gt
    need = topk - n_gt
    has_k = thr > -jnp.inf

    tlim_ref[...] = jnp.full((SUBLANES, Q_TILE), seq_len, I32)

    @pl.when(jnp.max(jnp.where(has_k, n_eq - need, 0)) > 0)
    def _():
        sub = lax.broadcasted_iota(I32, (SUBLANES, Q_TILE), 0)

        ties_ref[...] = jnp.zeros(ties_ref.shape, I32)

        def pack_chunk(c, slot, carry):
            eq = (sc_ref[pl.ds(chunk_start(c), K_TILE), :] == thr).reshape(
                KEY_BITS, SUBLANES, Q_TILE)
            word = jnp.zeros((SUBLANES, Q_TILE), I32)
            for g in range(KEY_BITS):
                word = word | jnp.where(eq[g], INT_MIN if g == 0 else 1 << (KEY_BITS - 1 - g), 0)
            ties_ref[c] = word
            return carry

        chunk_loop(pack_chunk, 0)
        ties = [ties_ref[c] for c in range(max_chunks)]

        def ties_before(lim):
            tot = jnp.zeros((SUBLANES, Q_TILE), I32)
            for c in range(max_chunks):
                g_last = (lim - (c * K_TILE + 1) - sub) >> 3
                below = jnp.where(
                    g_last < 0, 0,
                    lax.shift_left(jnp.int32(-1),
                                   KEY_BITS - 1 - jnp.minimum(g_last, KEY_BITS - 1)))
                tot = tot + lax.population_count(ties[c] & below)
            return jnp.sum(tot, axis=0, keepdims=True)

        def tie_step(t, lim):
            bit = lax.shift_left(jnp.int32(1), (seq_len.bit_length() - 2) - t)
            cand = lim | bit
            return jnp.where(ties_before(cand) < need, cand, lim)
        lim = lax.fori_loop(0, seq_len.bit_length() - 1, tie_step,
                            jnp.zeros((1, Q_TILE), I32))
        tlim_ref[...] = jnp.broadcast_to(jnp.where(n_eq > need, lim, seq_len),
                                         (SUBLANES, Q_TILE))

    tlim = tlim_ref[0:1, :]

    def logits_chunk(c, slot, m8):
        r0 = chunk_start(c)
        sc = sc_ref[pl.ds(r0, K_TILE), :]
        tie_ok = jnp.logical_and(sc == thr, (row_k + r0) <= tlim)
        sel = jnp.logical_and(rel <= (i - c) * K_TILE,
                              jnp.logical_or(sc > thr, tie_ok))
        bias_ref[slot] = jnp.where(sel, 0.0, -jnp.inf).astype(F32)
        out = []
        for h in range(N_HEADS):
            raw = l_ref[h, pl.ds(r0, K_TILE), :] if h < EARLY_QK_HEADS else qk_logits(h, r0)
            l = raw + bias_ref[slot]
            l_ref[h, pl.ds(r0, K_TILE), :] = l
            out.append(jnp.maximum(
                m8[h], jnp.max(l.reshape(K_TILE // SUBLANES, SUBLANES, Q_TILE), axis=0)))
        return tuple(out)

    m8 = chunk_loop(
        logits_chunk,
        tuple(jnp.full((SUBLANES, Q_TILE), -jnp.inf, F32) for _ in range(N_HEADS)))
    m_all = [jnp.max(m, axis=0, keepdims=True) for m in m8]

    oT_ref[...] = jnp.zeros(oT_ref.shape, F32)

    def pv_chunk(c, slot, s8):
        r0 = chunk_start(c)
        out = []
        for h in range(N_HEADS):
            hs = slice(h * HEAD_DIM, (h + 1) * HEAD_DIM)
            p = jnp.exp2(l_ref[h, pl.ds(r0, K_TILE), :] - m_all[h])
            out.append(s8[h] + fold_rows(p))
            oT_ref[hs, :] += jnp.dot(vT_ref[0, hs, pl.ds(r0, K_TILE)], p.astype(BF16),
                                     preferred_element_type=F32)
        return tuple(out)

    s8 = chunk_loop(
        pv_chunk, tuple(jnp.zeros((SUBLANES, Q_TILE), F32) for _ in range(N_HEADS)))
    for h in range(N_HEADS):
        hs = slice(h * HEAD_DIM, (h + 1) * HEAD_DIM)
        oT_ref[hs, :] = oT_ref[hs, :] / jnp.sum(s8[h], axis=0, keepdims=True)

    o_ref[0] = oT_ref[...].T.astype(BF16)


def _ffn_kernel(x_ref, attn_ref, ga_ref, mc_ref, wao_ref, wo_ref, g2_ref,
                wup_ref, fcw_ref, wdn_ref, gf_ref, out_ref,
                gbuf_ref, vbuf_ref, h1_ref, xn2_ref, merged_ref, *, tiles_per_seq):
    t = pl.program_id(0)
    tm = x_ref.shape[1]
    cur = lax.rem(t + 1, 2)
    nxt = lax.rem(t, 2)

    @pl.when(t == 0)
    def _():
        h1_ref[1] = jnp.zeros(h1_ref.shape[1:], F32)
        xn2_ref[1] = jnp.zeros(xn2_ref.shape[1:], BF16)

    @pl.when(jnp.logical_or(t == 0, lax.rem(t - 1, tiles_per_seq) == 0))
    def _():
        zeros = jnp.zeros((SUBLANES, FFN_CHUNK), F32)
        for n in range(D_FF // FFN_CHUNK):
            gbuf_ref[n, 0:SUBLANES, :] = zeros
            vbuf_ref[n, 0:SUBLANES, :] = zeros

    def merge_next():
        ba = jnp.dot(attn_ref[0], wao_ref[...], preferred_element_type=F32)
        return (ga_ref[0] * ba + mc_ref[0]).astype(BF16)

    def norm_next(merged):
        h1n = x_ref[0] + jnp.dot(merged, wo_ref[...], preferred_element_type=F32)
        h1_ref[nxt] = h1n
        xn2_ref[nxt] = _rms(h1n, g2_ref[...]).astype(BF16)

    xn2 = xn2_ref[cur]

    def causal_conv(u, buf_ref, col0):
        buf_ref[SUBLANES:SUBLANES + tm, :] = u
        u1 = buf_ref[SUBLANES - 1:SUBLANES - 1 + tm, :]
        u2 = buf_ref[SUBLANES - 2:SUBLANES - 2 + tm, :]
        buf_ref[0:SUBLANES, :] = u[tm - SUBLANES:, :]
        w = fcw_ref[:, col0:col0 + FFN_CHUNK]
        return u2 * w[0:1, :] + u1 * w[1:2, :] + u * w[2:3, :]

    def up_proj(n):
        c0 = n * FFN_CHUNK
        ug = jnp.dot(xn2, wup_ref[:, c0:c0 + FFN_CHUNK], preferred_element_type=F32)
        uv = jnp.dot(xn2, wup_ref[:, D_FF + c0:D_FF + c0 + FFN_CHUNK],
                     preferred_element_type=F32)
        return ug, uv

    nchunk = D_FF // FFN_CHUNK
    out_ref[0] = h1_ref[cur]
    ahead = [up_proj(n) for n in range(min(FFN_LOOKAHEAD, nchunk))]
    acts = []
    for n in range(nchunk):
        c0 = n * FFN_CHUNK
        ug, uv = ahead.pop(0)
        if n + FFN_LOOKAHEAD < nchunk:
            ahead.append(up_proj(n + FFN_LOOKAHEAD))
        if n == MERGE_AT_CHUNK:
            merged_ref[...] = merge_next()
        if n == NORM_AT_CHUNK:
            norm_next(merged_ref[...])
        gt = causal_conv(ug, gbuf_ref.at[n], c0)
        val = causal_conv(uv, vbuf_ref.at[n], D_FF + c0)
        acts.append((gt * jax.nn.sigmoid(gt) * val).astype(BF16))
        if len(acts) == DOWN_GROUP or n == nchunk - 1:
            r0 = (n + 1 - len(acts)) * FFN_CHUNK
            act = acts[0] if len(acts) == 1 else jnp.concatenate(acts, axis=1)
            out_ref[0] += jnp.dot(act, wdn_ref[r0:r0 + act.shape[1], :],
                                  preferred_element_type=F32)
            acts = []

    if NORM_AT_CHUNK >= nchunk:
        norm_next(merged_ref[...])
    out_ref[0] = _rms(out_ref[0], gf_ref[...])


def _sparse_attention(iqT, iwT, qT, ik, k, vT, *, topk, cast_along=()):
    B, _, S = qT.shape
    nq = S // Q_TILE
    featT = lambda c: pl.BlockSpec((1, c, Q_TILE), lambda b, j: (b, 0, j))

    def rows_spec(w):
        rows = next(r for r in range(BF16_ROWS, w.shape[0] + 1, BF16_ROWS)
                    if w.shape[0] % r == 0 and w.shape[0] // r <= B * nq)
        last = w.shape[0] // rows - 1
        return pl.BlockSpec((rows, w.shape[1]), lambda b, j: (jnp.minimum(b * nq + j, last), 0))

    cast_specs = [rows_spec(w) for w in cast_along]
    outs = pl.pallas_call(
        functools.partial(_attn_kernel, seq_len=S, topk=topk, n_cast=len(cast_along)),
        grid=(B, nq),
        in_specs=[featT(IDX_HEADS * IDX_DIM), featT(IDX_HEADS), featT(ATTN_WIDTH),
                  pl.BlockSpec((1, S, IDX_DIM), lambda b, j: (b, 0, 0)),
                  pl.BlockSpec((1, N_HEADS, S, HEAD_DIM), lambda b, j: (b, 0, 0, 0)),
                  pl.BlockSpec((1, ATTN_WIDTH, S), lambda b, j: (b, 0, 0))] + cast_specs,
        out_specs=[pl.BlockSpec((1, Q_TILE, ATTN_WIDTH), lambda b, j: (b, j, 0))] + cast_specs,
        out_shape=[jax.ShapeDtypeStruct((B, S, ATTN_WIDTH), BF16)]
        + [jax.ShapeDtypeStruct(w.shape, BF16) for w in cast_along],
        scratch_shapes=[pltpu.VMEM((S, Q_TILE), F32),
                        pltpu.VMEM((S // K_TILE, KEY_BITS, SUBLANES, Q_TILE), I32),
                        pltpu.VMEM((2, K_TILE, Q_TILE), F32),
                        pltpu.VMEM((N_HEADS, S, Q_TILE), F32),
                        pltpu.VMEM((ATTN_WIDTH, Q_TILE), F32),
                        pltpu.VMEM((SUBLANES, Q_TILE), I32),
                        pltpu.VMEM((S // K_TILE, SUBLANES, Q_TILE), I32)],
        compiler_params=pltpu.CompilerParams(
            dimension_semantics=("arbitrary", "arbitrary"), vmem_limit_bytes=VMEM_LIMIT),
        name="attn",
    )(iqT, iwT, qT, ik, k, vT, *cast_along)
    return outs[0], tuple(outs[1:])


def _const_spec(shape):
    nd = len(shape)
    return pl.BlockSpec(shape, lambda *_: (0,) * nd, pipeline_mode=pl.Buffered(1))


def kernel(x, norm_mix_g, w_in, b_gate, w_attn_out, conv_w, w_conv_out, w_o,
           norm_ffn_g, w_up, ffn_conv_w, w_down, norm_final_g):
    B, S, D = x.shape
    assert D == D_MODEL and S % PROJ_TILE == 0 and S % FFN_TILE == 0
    assert S % Q_TILE == 0 and Q_TILE == K_TILE
    assert norm_mix_g.shape[0] == 1, "single-layer block"
    topk = min(INDEX_TOPK_MAX, S // 4)
    tm = PROJ_TILE

    w = w_in[0]
    o_q, o_k, o_v, o_iq = 0, ATTN_WIDTH, 2 * ATTN_WIDTH, 3 * ATTN_WIDTH
    o_ik = o_iq + IDX_HEADS * IDX_DIM
    o_iw = o_ik + IDX_DIM
    o_ch = o_iw + IDX_HEADS
    o_g = o_ch + 3 * CONV_WIDTH
    wqT = w[:, o_q:o_k].T.astype(BF16)
    wvT = w[:, o_v:o_iq].T.astype(BF16)
    wiqT = w[:, o_iq:o_ik].T.astype(BF16)
    wiwT = jnp.pad(w[:, o_iw:o_ch].T, ((0, BF16_ROWS - IDX_HEADS), (0, 0))).astype(BF16)
    wk = w[:, o_k:o_v].astype(BF16)
    wik = jnp.pad(w[:, o_ik:o_iw], ((0, 0), (0, LANES - IDX_DIM))).astype(BF16)
    wconv = w[:, o_ch:o_g].astype(BF16)
    wg = w[:, o_g:].astype(BF16)
    g1 = norm_mix_g.reshape(1, D)
    bg = b_gate.reshape(1, 2 * D)
    cw = conv_w[0]
    wco = w_conv_out[0].astype(BF16)

    cparams = pltpu.CompilerParams(
        dimension_semantics=("arbitrary", "arbitrary"), vmem_limit_bytes=VMEM_LIMIT)

    tok = lambda c, t: pl.BlockSpec((1, t, c), lambda b, j: (b, j, 0))
    featT = lambda c, t: pl.BlockSpec((1, c, t), lambda b, j: (b, 0, j))

    k, ik, qT, vT, iqT, iwT, ga, mc = pl.pallas_call(
        _proj_kernel,
        grid=(B, S // tm),
        in_specs=[tok(D, tm), _const_spec((1, D)),
                  _const_spec(wk.shape), _const_spec(wik.shape),
                  _const_spec(wconv.shape), _const_spec(wg.shape),
                  _const_spec(wqT.shape), _const_spec(wvT.shape),
                  _const_spec(wiqT.shape), _const_spec(wiwT.shape),
                  _const_spec(bg.shape), _const_spec(cw.shape), _const_spec(wco.shape)],
        out_specs=[pl.BlockSpec((1, N_HEADS, tm, HEAD_DIM), lambda b, j: (b, 0, j, 0)),
                   tok(IDX_DIM, tm), featT(ATTN_WIDTH, tm), featT(ATTN_WIDTH, tm),
                   featT(IDX_HEADS * IDX_DIM, tm), featT(IDX_HEADS, tm),
                   tok(D, tm), tok(D, tm)],
        out_shape=[jax.ShapeDtypeStruct((B, N_HEADS, S, HEAD_DIM), BF16),
                   jax.ShapeDtypeStruct((B, S, IDX_DIM), BF16),
                   jax.ShapeDtypeStruct((B, ATTN_WIDTH, S), BF16),
                   jax.ShapeDtypeStruct((B, ATTN_WIDTH, S), BF16),
                   jax.ShapeDtypeStruct((B, IDX_HEADS * IDX_DIM, S), BF16),
                   jax.ShapeDtypeStruct((B, IDX_HEADS, S), F32),
                   jax.ShapeDtypeStruct((B, S, D), F32),
                   jax.ShapeDtypeStruct((B, S, D), F32)],
        scratch_shapes=[pltpu.VMEM((tm + SUBLANES, CONV_WIDTH), F32)],
        compiler_params=cparams,
        name="proj",
    )(x, g1, wk, wik, wconv, wg, wqT, wvT, wiqT, wiwT, bg, cw, wco)

    attn, (wao, wo, wup, wdn) = _sparse_attention(
        iqT, iwT, qT, ik, k, vT, topk=topk,
        cast_along=(w_attn_out[0], w_o[0], w_up[0], w_down[0]))
    fcw = ffn_conv_w[0]
    g2 = norm_ffn_g.reshape(1, D)
    gf = norm_final_g.reshape(1, D)
    nchunk = D_FF // FFN_CHUNK
    assert nchunk * FFN_CHUNK == D_FF
    tf = FFN_TILE

    nt = S // tf
    n_tiles = B * nt

    def tile_in(c):
        def index(t):
            tt = jnp.minimum(t, n_tiles - 1)
            return (tt // nt, tt % nt, 0)
        return pl.BlockSpec((1, tf, c), index)

    def tile_out(t):
        tt = jnp.maximum(t - 1, 0)
        return (tt // nt, tt % nt, 0)

    out = pl.pallas_call(
        functools.partial(_ffn_kernel, tiles_per_seq=nt),
        grid=(n_tiles + 1,),
        in_specs=[tile_in(D), tile_in(ATTN_WIDTH), tile_in(D), tile_in(D),
                  _const_spec(wao.shape), _const_spec(wo.shape), _const_spec(g2.shape),
                  _const_spec(wup.shape), _const_spec(fcw.shape), _const_spec(wdn.shape),
                  _const_spec(gf.shape)],
        out_specs=pl.BlockSpec((1, tf, D), tile_out),
        out_shape=jax.ShapeDtypeStruct((B, S, D), F32),
        scratch_shapes=[pltpu.VMEM((nchunk, tf + SUBLANES, FFN_CHUNK), F32),
                        pltpu.VMEM((nchunk, tf + SUBLANES, FFN_CHUNK), F32),
                        pltpu.VMEM((2, tf, D), F32),
                        pltpu.VMEM((2, tf, D), BF16),
                        pltpu.VMEM((tf, D), BF16)],
        compiler_params=pltpu.CompilerParams(
            dimension_semantics=("arbitrary",), vmem_limit_bytes=VMEM_LIMIT),
        name="ffn",
    )(x, attn, ga, mc, wao, wo, g2, wup, fcw, wdn, gf)
    return out
```

```python
import functools

import jax
import jax.numpy as jnp
from jax import lax
from jax.experimental import pallas as pl
from jax.experimental.pallas import tpu as pltpu

D_MODEL = 1024
N_HEADS = 8
HEAD_DIM = 64
ATTN_WIDTH = N_HEADS * HEAD_DIM
IDX_HEADS = 8
IDX_DIM = 64
INDEX_TOPK_MAX = 256
CONV_WIDTH = 512
D_FF = 2816
EPS = 1e-6
LOG2E = 1.4426950408889634

F32 = jnp.float32
BF16 = jnp.bfloat16
I32 = jnp.int32
KEY_BITS = 32
INT_MIN = -(2 ** 31)

SUBLANES = 8
BF16_ROWS = 16
LANES = 128
MXU_DIM = 256
VMEM_LIMIT = 56 * 1024 * 1024

PROJ_TILE = 512
FFN_TILE = 256
Q_TILE = MXU_DIM
K_TILE = 256
SCORE_ROWS = 64
EARLY_QK_HEADS = 4
FFN_CHUNK = 256
FFN_LOOKAHEAD = 2
DOWN_GROUP = 4
MERGE_AT_CHUNK = 8
NORM_AT_CHUNK = 10

_NT = (((1,), (1,)), ((), ()))


def _rms(x, g):
    return x * lax.rsqrt(jnp.mean(x * x, axis=-1, keepdims=True) + EPS) * g


def _shifted_rows(buf_ref, u, first_tile):
    tm = u.shape[0]

    @pl.when(first_tile)
    def _():
        buf_ref[0:SUBLANES, :] = jnp.zeros((SUBLANES, u.shape[1]), F32)

    buf_ref[SUBLANES:SUBLANES + tm, :] = u
    u1 = buf_ref[SUBLANES - 1:SUBLANES - 1 + tm, :]
    u2 = buf_ref[SUBLANES - 2:SUBLANES - 2 + tm, :]
    buf_ref[0:SUBLANES, :] = u[tm - SUBLANES:, :]
    return u1, u2


def _proj_kernel(x_ref, g_ref, wk_ref, wik_ref, wconv_ref, wg_ref,
                 wqT_ref, wvT_ref, wiqT_ref, wiwT_ref,
                 bg_ref, cw_ref, wco_ref,
                 k_ref, ik_ref, qT_ref, vT_ref, iqT_ref, iwT_ref, ga_ref, mc_ref,
                 ubuf_ref):
    j = pl.program_id(1)
    xn = _rms(x_ref[0], g_ref[...]).astype(BF16)

    pc = jnp.dot(xn, wconv_ref[...], preferred_element_type=F32)
    ch = pc[:, 0:CONV_WIDTH]
    cb = pc[:, CONV_WIDTH:2 * CONV_WIDTH]
    cc = pc[:, 2 * CONV_WIDTH:3 * CONV_WIDTH]
    u = cc * ch
    u1, u2 = _shifted_rows(ubuf_ref, u, j == 0)
    cw = cw_ref[...]
    conv = u2 * cw[0:1, :] + u1 * cw[1:2, :] + u * cw[2:3, :]
    cbconv = (cb * conv).astype(BF16)

    g = jax.nn.sigmoid(jnp.dot(xn, wg_ref[...], preferred_element_type=F32)
                       + bg_ref[...])
    ga_ref[0] = g[:, 0:D_MODEL]

    qT = lax.dot_general(wqT_ref[...], xn, _NT, preferred_element_type=F32)
    qT_ref[0] = (qT * (HEAD_DIM ** -0.5 * LOG2E)).astype(BF16)
    vT = lax.dot_general(wvT_ref[...], xn, _NT, preferred_element_type=F32)
    vT_ref[0] = vT.astype(BF16)
    iqT = lax.dot_general(wiqT_ref[...], xn, _NT, preferred_element_type=F32)
    iqT_ref[0] = iqT.astype(BF16)
    iwT = lax.dot_general(wiwT_ref[...], xn, _NT, preferred_element_type=F32)
    iwT_ref[0] = iwT[0:IDX_HEADS, :]

    kk = jnp.dot(xn, wk_ref[...], preferred_element_type=F32)
    for h in range(N_HEADS):
        k_ref[0, h] = kk[:, h * HEAD_DIM:(h + 1) * HEAD_DIM].astype(BF16)
    ik = jnp.dot(xn, wik_ref[...], preferred_element_type=F32)
    ik_ref[0] = ik[:, 0:IDX_DIM].astype(BF16)

    bc = jnp.dot(cbconv, wco_ref[...], preferred_element_type=F32)
    mc_ref[0] = g[:, D_MODEL:2 * D_MODEL] * bc


def _bit_transpose32(a):
    rows = [a[j] for j in range(KEY_BITS)]
    j, m = KEY_BITS // 2, 0x0000FFFF
    while j:
        for k in range(KEY_BITS):
            if k & j:
                continue
            t = (rows[k] ^ lax.shift_right_logical(rows[k + j], jnp.int32(j))) & m
            rows[k] = rows[k] ^ t
            rows[k + j] = rows[k + j] ^ lax.shift_left(t, jnp.int32(j))
        j >>= 1
        m = (m ^ (m << j)) & 0xFFFFFFFF
    return jnp.stack(rows)


def _attn_kernel(iqT_ref, iwT_ref, qT_ref, ik_ref, k_ref, vT_ref, *rest, seq_len, topk, n_cast):
    cast_in, (o_ref, *rest) = rest[:n_cast], rest[n_cast:]
    cast_out, scratch = rest[:n_cast], rest[n_cast:]
    sc_ref, planes_ref, bias_ref, l_ref, oT_ref, tlim_ref, ties_ref = scratch
    for src, dst in zip(cast_in, cast_out):
        dst[...] = src[...].astype(BF16)
    assert K_TILE == KEY_BITS * SUBLANES
    max_chunks = seq_len // K_TILE
    i = pl.program_id(1)
    nch = i + 1
    row_k = lax.broadcasted_iota(I32, (K_TILE, Q_TILE), 0)
    lane_q = lax.broadcasted_iota(I32, (K_TILE, Q_TILE), 1)
    rel = row_k - lane_q

    def chunk_start(c):
        return pl.multiple_of(c * K_TILE, K_TILE)

    def chunk_loop(body, init):
        def pair(p, carry):
            return body(2 * p + 1, 1, body(2 * p, 0, carry))
        carry = lax.fori_loop(0, lax.shift_right_logical(nch, 1), pair, init)
        return lax.cond((nch & 1) == 1, lambda cr: body(nch - 1, 0, cr), lambda cr: cr, carry)

    def qk_logits(h, r0):
        return jnp.dot(k_ref[0, h, pl.ds(r0, K_TILE), :],
                       qT_ref[0, h * HEAD_DIM:(h + 1) * HEAD_DIM, :],
                       preferred_element_type=F32)

    def fold_rows(m):
        return jnp.sum(m.reshape(K_TILE // SUBLANES, SUBLANES, Q_TILE), axis=0)

    iw = iwT_ref[0] * ((IDX_HEADS ** -0.5) * (IDX_DIM ** -0.5))

    def score_chunk(c, slot, carry):
        for sb in range(K_TILE // SCORE_ROWS):
            r0 = pl.multiple_of(c * K_TILE + sb * SCORE_ROWS, SCORE_ROWS)
            ikc = ik_ref[0, pl.ds(r0, SCORE_ROWS), :]
            acc = jnp.zeros((SCORE_ROWS, Q_TILE), F32)
            for h in range(IDX_HEADS):
                iq_h = iqT_ref[0, h * IDX_DIM:(h + 1) * IDX_DIM, :]
                d = jnp.dot(ikc, iq_h, preferred_element_type=F32)
                acc = acc + jnp.maximum(d, 0.0) * iw[h:h + 1, :]
            acc = jnp.where(acc == 0.0, 0.0, acc)
            valid = (rel[0:SCORE_ROWS, :] + sb * SCORE_ROWS) <= (i - c) * K_TILE
            sc_ref[pl.ds(r0, SCORE_ROWS), :] = jnp.where(valid, acc, -jnp.inf)
        for h in range(EARLY_QK_HEADS):
            l_ref[h, pl.ds(chunk_start(c), K_TILE), :] = qk_logits(h, chunk_start(c))
        for lt in range(Q_TILE // LANES):
            ls = slice(lt * LANES, (lt + 1) * LANES)
            bits = pltpu.bitcast(sc_ref[pl.ds(chunk_start(c), K_TILE), ls], I32)
            u = bits ^ ((bits >> 31) | INT_MIN)
            planes_ref[c, :, :, ls] = _bit_transpose32(u.reshape(KEY_BITS, SUBLANES, LANES))
        return carry

    @pl.when(jnp.logical_and(pl.program_id(0) == 0, i == 0))
    def _():
        planes_ref[...] = jnp.zeros(planes_ref.shape, I32)

    chunk_loop(score_chunk, 0)

    def count(words):
        pops = [lax.population_count(w) for w in words]
        while len(pops) > 1:
            pops = [a + b for a, b in zip(pops[0::2], pops[1::2])]
        return jnp.sum(pops[0], axis=0, keepdims=True)

    def radix_step(r, carry):
        match, n_gt, thr_u = carry
        ones = [match[c] & planes_ref[c, r] for c in range(max_chunks)]
        n_one = count(ones)
        take = (n_gt + n_one) >= topk
        thr_u = jnp.where(take, thr_u | lax.shift_left(jnp.int32(1), KEY_BITS - 1 - r), thr_u)
        n_gt = jnp.where(take, n_gt, n_gt + n_one)
        match = tuple(jnp.where(take, ones[c], match[c] ^ ones[c]) for c in range(max_chunks))
        return match, n_gt, thr_u

    match0 = tuple(jnp.broadcast_to(jnp.where(c < nch, -1, 0), (SUBLANES, Q_TILE)).astype(I32)
                   for c in range(max_chunks))
    match, n_gt, thr_u = lax.fori_loop(
        0, KEY_BITS, radix_step,
        (match0, jnp.zeros((1, Q_TILE), I32), jnp.zeros((1, Q_TILE), I32)))
    key = thr_u ^ INT_MIN
    thr = pltpu.bitcast(key ^ ((key >> 31) & 0x7FFFFFFF), F32)
    thr = jnp.where(thr > -jnp.inf, thr, -jnp.inf)

    def rank_counts(t):
        def body(c, slot, acc):
            sc = sc_ref[pl.ds(chunk_start(c), K_TILE), :]
            code = jnp.where(sc > t, 1 << 16, jnp.where(sc == t, 1, 0)).astype(I32)
            return acc + fold_rows(code)
        both = jnp.sum(chunk_loop(body, jnp.zeros((SUBLANES, Q_TILE), I32)),
                       axis=0, keepdims=True)
        gt = both >> 16
        return gt, gt + (both & 0xFFFF)

    def off_by(state):
        _, gt, ge, steps = state
        bad = jnp.logical_or(gt >= topk, ge < topk)
        return jnp.logical_and(jnp.max(jnp.where(bad, 1, 0)) > 0, steps < topk)

    def walk(state):
        t, gt, ge, steps = state

        lower = lambda a, b: jnp.where(a < b, a, b)
        upper = lambda a, b: jnp.where(a > b, a, b)

        def fold(op, x):
            while x.shape[0] > 1:
                half = x.shape[0] // 2
                x = op(x[:half], x[half:])
            return x

        def body(c, slot, carry):
            sc = sc_ref[pl.ds(chunk_start(c), K_TILE), :]
            return (lower(carry[0], fold(lower, jnp.where(sc > t, sc, jnp.inf))),
                    upper(carry[1], fold(upper, jnp.where(sc < t, sc, -jnp.inf))))
        up, dn = chunk_loop(body, (jnp.full((1, Q_TILE), jnp.inf, F32),
                                   jnp.full((1, Q_TILE), -jnp.inf, F32)))
        t = jnp.where(gt >= topk, up, jnp.where(ge < topk, dn, t))
        return (t,) + rank_counts(t) + (steps + 1,)

    thr, n_gt, n_ge, _ = lax.while_loop(off_by, walk, (thr,) + rank_counts(thr) + (jnp.int32(0),))
    n_eq = n_ge - n_gt
    need = topk - n_gt
    has_k = thr > -jnp.inf
    thr = jnp.where(has_k, thr, jnp.finfo(F32).min)

    tlim_ref[...] = jnp.full((SUBLANES, Q_TILE), seq_len, I32)

    @pl.when(jnp.max(jnp.where(has_k, n_eq - need, 0)) > 0)
    def _():
        sub = lax.broadcasted_iota(I32, (SUBLANES, Q_TILE), 0)

        ties_ref[...] = jnp.zeros(ties_ref.shape, I32)

        def pack_chunk(c, slot, carry):
            eq = (sc_ref[pl.ds(chunk_start(c), K_TILE), :] == thr).reshape(
                KEY_BITS, SUBLANES, Q_TILE)
            word = jnp.zeros((SUBLANES, Q_TILE), I32)
            for g in range(KEY_BITS):
                word = word | jnp.where(eq[g], INT_MIN if g == 0 else 1 << (KEY_BITS - 1 - g), 0)
            ties_ref[c] = word
            return carry

        chunk_loop(pack_chunk, 0)
        ties = [ties_ref[c] for c in range(max_chunks)]

        def ties_before(lim):
            tot = jnp.zeros((SUBLANES, Q_TILE), I32)
            for c in range(max_chunks):
                g_last = (lim - (c * K_TILE + 1) - sub) >> 3
                below = jnp.where(
                    g_last < 0, 0,
                    lax.shift_left(jnp.int32(-1),
                                   KEY_BITS - 1 - jnp.minimum(g_last, KEY_BITS - 1)))
                tot = tot + lax.population_count(ties[c] & below)
            return jnp.sum(tot, axis=0, keepdims=True)

        def tie_step(t, lim):
            bit = lax.shift_left(jnp.int32(1), (seq_len.bit_length() - 2) - t)
            cand = lim | bit
            return jnp.where(ties_before(cand) < need, cand, lim)
        lim = lax.fori_loop(0, seq_len.bit_length() - 1, tie_step,
                            jnp.zeros((1, Q_TILE), I32))
        tlim_ref[...] = jnp.broadcast_to(jnp.where(n_eq > need, lim, seq_len),
                                         (SUBLANES, Q_TILE))

    tlim = tlim_ref[0:1, :]

    def logits_chunk(c, slot, m8):
        r0 = chunk_start(c)
        sc = sc_ref[pl.ds(r0, K_TILE), :]
        tie_ok = jnp.logical_and(sc == thr, (row_k + r0) <= tlim)
        sel = jnp.logical_or(sc > thr, tie_ok)
        bias_ref[slot] = jnp.where(sel, 0.0, -jnp.inf).astype(F32)
        out = []
        for h in range(N_HEADS):
            raw = l_ref[h, pl.ds(r0, K_TILE), :] if h < EARLY_QK_HEADS else qk_logits(h, r0)
            l = raw + bias_ref[slot]
            l_ref[h, pl.ds(r0, K_TILE), :] = l
            out.append(jnp.maximum(
                m8[h], jnp.max(l.reshape(K_TILE // SUBLANES, SUBLANES, Q_TILE), axis=0)))
        return tuple(out)

    m8 = chunk_loop(
        logits_chunk,
        tuple(jnp.full((SUBLANES, Q_TILE), -jnp.inf, F32) for _ in range(N_HEADS)))
    m_all = [jnp.max(m, axis=0, keepdims=True) for m in m8]

    oT_ref[...] = jnp.zeros(oT_ref.shape, F32)

    def pv_chunk(c, slot, s8):
        r0 = chunk_start(c)
        out = []
        for h in range(N_HEADS):
            hs = slice(h * HEAD_DIM, (h + 1) * HEAD_DIM)
            p = jnp.exp2(l_ref[h, pl.ds(r0, K_TILE), :] - m_all[h])
            out.append(s8[h] + fold_rows(p))
            oT_ref[hs, :] += jnp.dot(vT_ref[0, hs, pl.ds(r0, K_TILE)], p.astype(BF16),
                                     preferred_element_type=F32)
        return tuple(out)

    s8 = chunk_loop(
        pv_chunk, tuple(jnp.zeros((SUBLANES, Q_TILE), F32) for _ in range(N_HEADS)))
    for h in range(N_HEADS):
        hs = slice(h * HEAD_DIM, (h + 1) * HEAD_DIM)
        oT_ref[hs, :] = oT_ref[hs, :] / jnp.sum(s8[h], axis=0, keepdims=True)

    o_ref[0] = oT_ref[...].T.astype(BF16)


def _ffn_kernel(x_ref, attn_ref, ga_ref, mc_ref, wao_ref, wo_ref, g2_ref,
                wup_ref, fcw_ref, wdn_ref, gf_ref, out_ref,
                gbuf_ref, vbuf_ref, h1_ref, xn2_ref, merged_ref, *, tiles_per_seq):
    t = pl.program_id(0)
    tm = x_ref.shape[1]
    cur = lax.rem(t + 1, 2)
    nxt = lax.rem(t, 2)

    @pl.when(t == 0)
    def _():
        h1_ref[1] = jnp.zeros(h1_ref.shape[1:], F32)
        xn2_ref[1] = jnp.zeros(xn2_ref.shape[1:], BF16)

    @pl.when(jnp.logical_or(t == 0, lax.rem(t - 1, tiles_per_seq) == 0))
    def _():
        zeros = jnp.zeros((SUBLANES, FFN_CHUNK), F32)
        for n in range(D_FF // FFN_CHUNK):
            gbuf_ref[n, 0:SUBLANES, :] = zeros
            vbuf_ref[n, 0:SUBLANES, :] = zeros

    def merge_next():
        ba = jnp.dot(attn_ref[0], wao_ref[...], preferred_element_type=F32)
        return (ga_ref[0] * ba + mc_ref[0]).astype(BF16)

    def norm_next(merged):
        h1n = x_ref[0] + jnp.dot(merged, wo_ref[...], preferred_element_type=F32)
        h1_ref[nxt] = h1n
        xn2_ref[nxt] = _rms(h1n, g2_ref[...]).astype(BF16)

    xn2 = xn2_ref[cur]

    def causal_conv(u, buf_ref, col0):
        buf_ref[SUBLANES:SUBLANES + tm, :] = u
        u1 = buf_ref[SUBLANES - 1:SUBLANES - 1 + tm, :]
        u2 = buf_ref[SUBLANES - 2:SUBLANES - 2 + tm, :]
        buf_ref[0:SUBLANES, :] = u[tm - SUBLANES:, :]
        w = fcw_ref[:, col0:col0 + FFN_CHUNK]
        return u2 * w[0:1, :] + u1 * w[1:2, :] + u * w[2:3, :]

    def up_proj(n):
        c0 = n * FFN_CHUNK
        ug = jnp.dot(xn2, wup_ref[:, c0:c0 + FFN_CHUNK], preferred_element_type=F32)
        uv = jnp.dot(xn2, wup_ref[:, D_FF + c0:D_FF + c0 + FFN_CHUNK],
                     preferred_element_type=F32)
        return ug, uv

    nchunk = D_FF // FFN_CHUNK
    out_ref[0] = h1_ref[cur]
    ahead = [up_proj(n) for n in range(min(FFN_LOOKAHEAD, nchunk))]
    acts = []
    for n in range(nchunk):
        c0 = n * FFN_CHUNK
        ug, uv = ahead.pop(0)
        if n + FFN_LOOKAHEAD < nchunk:
            ahead.append(up_proj(n + FFN_LOOKAHEAD))
        if n == MERGE_AT_CHUNK:
            merged_ref[...] = merge_next()
        if n == NORM_AT_CHUNK:
            norm_next(merged_ref[...])
        gt = causal_conv(ug, gbuf_ref.at[n], c0)
        val = causal_conv(uv, vbuf_ref.at[n], D_FF + c0)
        acts.append((gt * jax.nn.sigmoid(gt) * val).astype(BF16))
        if len(acts) == DOWN_GROUP or n == nchunk - 1:
            r0 = (n + 1 - len(acts)) * FFN_CHUNK
            act = acts[0] if len(acts) == 1 else jnp.concatenate(acts, axis=1)
            out_ref[0] += jnp.dot(act, wdn_ref[r0:r0 + act.shape[1], :],
                                  preferred_element_type=F32)
            acts = []

    if NORM_AT_CHUNK >= nchunk:
        norm_next(merged_ref[...])
    out_ref[0] = _rms(out_ref[0], gf_ref[...])


def _sparse_attention(iqT, iwT, qT, ik, k, vT, *, topk, cast_along=()):
    B, _, S = qT.shape
    nq = S // Q_TILE
    featT = lambda c: pl.BlockSpec((1, c, Q_TILE), lambda b, j: (b, 0, j))

    def rows_spec(w):
        rows = next(r for r in range(BF16_ROWS, w.shape[0] + 1, BF16_ROWS)
                    if w.shape[0] % r == 0 and w.shape[0] // r <= B * nq)
        last = w.shape[0] // rows - 1
        return pl.BlockSpec((rows, w.shape[1]), lambda b, j: (jnp.minimum(b * nq + j, last), 0))

    cast_specs = [rows_spec(w) for w in cast_along]
    outs = pl.pallas_call(
        functools.partial(_attn_kernel, seq_len=S, topk=topk, n_cast=len(cast_along)),
        grid=(B, nq),
        in_specs=[featT(IDX_HEADS * IDX_DIM), featT(IDX_HEADS), featT(ATTN_WIDTH),
                  pl.BlockSpec((1, S, IDX_DIM), lambda b, j: (b, 0, 0)),
                  pl.BlockSpec((1, N_HEADS, S, HEAD_DIM), lambda b, j: (b, 0, 0, 0)),
                  pl.BlockSpec((1, ATTN_WIDTH, S), lambda b, j: (b, 0, 0))] + cast_specs,
        out_specs=[pl.BlockSpec((1, Q_TILE, ATTN_WIDTH), lambda b, j: (b, j, 0))] + cast_specs,
        out_shape=[jax.ShapeDtypeStruct((B, S, ATTN_WIDTH), BF16)]
        + [jax.ShapeDtypeStruct(w.shape, BF16) for w in cast_along],
        scratch_shapes=[pltpu.VMEM((S, Q_TILE), F32),
                        pltpu.VMEM((S // K_TILE, KEY_BITS, SUBLANES, Q_TILE), I32),
                        pltpu.VMEM((2, K_TILE, Q_TILE), F32),
                        pltpu.VMEM((N_HEADS, S, Q_TILE), F32),
                        pltpu.VMEM((ATTN_WIDTH, Q_TILE), F32),
                        pltpu.VMEM((SUBLANES, Q_TILE), I32),
                        pltpu.VMEM((S // K_TILE, SUBLANES, Q_TILE), I32)],
        compiler_params=pltpu.CompilerParams(
            dimension_semantics=("arbitrary", "arbitrary"), vmem_limit_bytes=VMEM_LIMIT),
        name="attn",
    )(iqT, iwT, qT, ik, k, vT, *cast_along)
    return outs[0], tuple(outs[1:])


def _const_spec(shape):
    nd = len(shape)
    return pl.BlockSpec(shape, lambda *_: (0,) * nd, pipeline_mode=pl.Buffered(1))


def kernel(x, norm_mix_g, w_in, b_gate, w_attn_out, conv_w, w_conv_out, w_o,
           norm_ffn_g, w_up, ffn_conv_w, w_down, norm_final_g):
    B, S, D = x.shape
    assert D == D_MODEL and S % PROJ_TILE == 0 and S % FFN_TILE == 0
    assert S % Q_TILE == 0 and Q_TILE == K_TILE
    assert norm_mix_g.shape[0] == 1, "single-layer block"
    topk = min(INDEX_TOPK_MAX, S // 4)
    tm = PROJ_TILE

    w = w_in[0]
    o_q, o_k, o_v, o_iq = 0, ATTN_WIDTH, 2 * ATTN_WIDTH, 3 * ATTN_WIDTH
    o_ik = o_iq + IDX_HEADS * IDX_DIM
    o_iw = o_ik + IDX_DIM
    o_ch = o_iw + IDX_HEADS
    o_g = o_ch + 3 * CONV_WIDTH
    wqT = w[:, o_q:o_k].T.astype(BF16)
    wvT = w[:, o_v:o_iq].T.astype(BF16)
    wiqT = w[:, o_iq:o_ik].T.astype(BF16)
    wiwT = jnp.pad(w[:, o_iw:o_ch].T, ((0, BF16_ROWS - IDX_HEADS), (0, 0))).astype(BF16)
    wk = w[:, o_k:o_v].astype(BF16)
    wik = jnp.pad(w[:, o_ik:o_iw], ((0, 0), (0, LANES - IDX_DIM))).astype(BF16)
    wconv = w[:, o_ch:o_g].astype(BF16)
    wg = w[:, o_g:].astype(BF16)
    g1 = norm_mix_g.reshape(1, D)
    bg = b_gate.reshape(1, 2 * D)
    cw = conv_w[0]
    wco = w_conv_out[0].astype(BF16)

    cparams = pltpu.CompilerParams(
        dimension_semantics=("arbitrary", "arbitrary"), vmem_limit_bytes=VMEM_LIMIT)

    tok = lambda c, t: pl.BlockSpec((1, t, c), lambda b, j: (b, j, 0))
    featT = lambda c, t: pl.BlockSpec((1, c, t), lambda b, j: (b, 0, j))

    k, ik, qT, vT, iqT, iwT, ga, mc = pl.pallas_call(
        _proj_kernel,
        grid=(B, S // tm),
        in_specs=[tok(D, tm), _const_spec((1, D)),
                  _const_spec(wk.shape), _const_spec(wik.shape),
                  _const_spec(wconv.shape), _const_spec(wg.shape),
                  _const_spec(wqT.shape), _const_spec(wvT.shape),
                  _const_spec(wiqT.shape), _const_spec(wiwT.shape),
                  _const_spec(bg.shape), _const_spec(cw.shape), _const_spec(wco.shape)],
        out_specs=[pl.BlockSpec((1, N_HEADS, tm, HEAD_DIM), lambda b, j: (b, 0, j, 0)),
                   tok(IDX_DIM, tm), featT(ATTN_WIDTH, tm), featT(ATTN_WIDTH, tm),
                   featT(IDX_HEADS * IDX_DIM, tm), featT(IDX_HEADS, tm),
                   tok(D, tm), tok(D, tm)],
        out_shape=[jax.ShapeDtypeStruct((B, N_HEADS, S, HEAD_DIM), BF16),
                   jax.ShapeDtypeStruct((B, S, IDX_DIM), BF16),
                   jax.ShapeDtypeStruct((B, ATTN_WIDTH, S), BF16),
                   jax.ShapeDtypeStruct((B, ATTN_WIDTH, S), BF16),
                   jax.ShapeDtypeStruct((B, IDX_HEADS * IDX_DIM, S), BF16),
                   jax.ShapeDtypeStruct((B, IDX_HEADS, S), F32),
                   jax.ShapeDtypeStruct((B, S, D), F32),
                   jax.ShapeDtypeStruct((B, S, D), F32)],
        scratch_shapes=[pltpu.VMEM((tm + SUBLANES, CONV_WIDTH), F32)],
        compiler_params=cparams,
        name="proj",
    )(x, g1, wk, wik, wconv, wg, wqT, wvT, wiqT, wiwT, bg, cw, wco)

    attn, (wao, wo, wup, wdn) = _sparse_attention(
        iqT, iwT, qT, ik, k, vT, topk=topk,
        cast_along=(w_attn_out[0], w_o[0], w_up[0], w_down[0]))
    fcw = ffn_conv_w[0]
    g2 = norm_ffn_g.reshape(1, D)
    gf = norm_final_g.reshape(1, D)
    nchunk = D_FF // FFN_CHUNK
    assert nchunk * FFN_CHUNK == D_FF
    tf = FFN_TILE

    nt = S // tf
    n_tiles = B * nt

    def tile_in(c):
        def index(t):
            tt = jnp.minimum(t, n_tiles - 1)
            return (tt // nt, tt % nt, 0)
        return pl.BlockSpec((1, tf, c), index)

    def tile_out(t):
        tt = jnp.maximum(t - 1, 0)
        return (tt // nt, tt % nt, 0)

    out = pl.pallas_call(
        functools.partial(_ffn_kernel, tiles_per_seq=nt),
        grid=(n_tiles + 1,),
        in_specs=[tile_in(D), tile_in(ATTN_WIDTH), tile_in(D), tile_in(D),
                  _const_spec(wao.shape), _const_spec(wo.shape), _const_spec(g2.shape),
                  _const_spec(wup.shape), _const_spec(fcw.shape), _const_spec(wdn.shape),
                  _const_spec(gf.shape)],
        out_specs=pl.BlockSpec((1, tf, D), tile_out),
        out_shape=jax.ShapeDtypeStruct((B, S, D), F32),
        scratch_shapes=[pltpu.VMEM((nchunk, tf + SUBLANES, FFN_CHUNK), F32),
                        pltpu.VMEM((nchunk, tf + SUBLANES, FFN_CHUNK), F32),
                        pltpu.VMEM((2, tf, D), F32),
                        pltpu.VMEM((2, tf, D), BF16),
                        pltpu.VMEM((tf, D), BF16)],
        compiler_params=pltpu.CompilerParams(
            dimension_semantics=("arbitrary",), vmem_limit_bytes=VMEM_LIMIT),
        name="ffn",
    )(x, attn, ga, mc, wao, wo, g2, wup, fcw, wdn, gf)
    return out
```

```python
import functools

import jax
import jax.numpy as jnp
from jax import lax
from jax.experimental import pallas as pl
from jax.experimental.pallas import tpu as pltpu

D_MODEL = 1024
N_HEADS = 8
HEAD_DIM = 64
ATTN_WIDTH = N_HEADS * HEAD_DIM
IDX_HEADS = 8
IDX_DIM = 64
INDEX_TOPK_MAX = 256
CONV_WIDTH = 512
D_FF = 2816
EPS = 1e-6
LOG2E = 1.4426950408889634

F32 = jnp.float32
BF16 = jnp.bfloat16
I32 = jnp.int32
KEY_BITS = 32
INT_MIN = -(2 ** 31)

SUBLANES = 8
BF16_ROWS = 16
LANES = 128
MXU_DIM = 256
VMEM_LIMIT = 56 * 1024 * 1024

PROJ_TILE = 512
FFN_TILE = 256
Q_TILE = MXU_DIM
K_TILE = 256
SCORE_ROWS = 64
EARLY_QK_HEADS = 4
FFN_CHUNK = 256
FFN_LOOKAHEAD = 2
DOWN_GROUP = 4
MERGE_AT_CHUNK = 8
NORM_AT_CHUNK = 10

_NT = (((1,), (1,)), ((), ()))


def _rms(x, g):
    return x * lax.rsqrt(jnp.mean(x * x, axis=-1, keepdims=True) + EPS) * g


def _shifted_rows(buf_ref, u, first_tile):
    tm = u.shape[0]

    @pl.when(first_tile)
    def _():
        buf_ref[0:SUBLANES, :] = jnp.zeros((SUBLANES, u.shape[1]), F32)

    buf_ref[SUBLANES:SUBLANES + tm, :] = u
    u1 = buf_ref[SUBLANES - 1:SUBLANES - 1 + tm, :]
    u2 = buf_ref[SUBLANES - 2:SUBLANES - 2 + tm, :]
    buf_ref[0:SUBLANES, :] = u[tm - SUBLANES:, :]
    return u1, u2


def _proj_kernel(x_ref, g_ref, wk_ref, wik_ref, wconv_ref, wg_ref,
                 wqT_ref, wvT_ref, wiqT_ref, wiwT_ref,
                 bg_ref, cw_ref, wco_ref,
                 k_ref, ik_ref, qT_ref, vT_ref, iqT_ref, iwT_ref, ga_ref, mc_ref,
                 ubuf_ref):
    j = pl.program_id(1)
    xn = _rms(x_ref[0], g_ref[...]).astype(BF16)

    pc = jnp.dot(xn, wconv_ref[...], preferred_element_type=F32)
    ch = pc[:, 0:CONV_WIDTH]
    cb = pc[:, CONV_WIDTH:2 * CONV_WIDTH]
    cc = pc[:, 2 * CONV_WIDTH:3 * CONV_WIDTH]
    u = cc * ch
    u1, u2 = _shifted_rows(ubuf_ref, u, j == 0)
    cw = cw_ref[...]
    conv = u2 * cw[0:1, :] + u1 * cw[1:2, :] + u * cw[2:3, :]
    cbconv = (cb * conv).astype(BF16)

    g = jax.nn.sigmoid(jnp.dot(xn, wg_ref[...], preferred_element_type=F32)
                       + bg_ref[...])
    ga_ref[0] = g[:, 0:D_MODEL]

    qT = lax.dot_general(wqT_ref[...], xn, _NT, preferred_element_type=F32)
    qT_ref[0] = (qT * (HEAD_DIM ** -0.5 * LOG2E)).astype(BF16)
    vT = lax.dot_general(wvT_ref[...], xn, _NT, preferred_element_type=F32)
    vT_ref[0] = vT.astype(BF16)
    iqT = lax.dot_general(wiqT_ref[...], xn, _NT, preferred_element_type=F32)
    iqT_ref[0] = iqT.astype(BF16)
    iwT = lax.dot_general(wiwT_ref[...], xn, _NT, preferred_element_type=F32)
    iwT_ref[0] = iwT[0:IDX_HEADS, :]

    kk = jnp.dot(xn, wk_ref[...], preferred_element_type=F32)
    for h in range(N_HEADS):
        k_ref[0, h] = kk[:, h * HEAD_DIM:(h + 1) * HEAD_DIM].astype(BF16)
    ik = jnp.dot(xn, wik_ref[...], preferred_element_type=F32)
    ik_ref[0] = ik[:, 0:IDX_DIM].astype(BF16)

    bc = jnp.dot(cbconv, wco_ref[...], preferred_element_type=F32)
    mc_ref[0] = g[:, D_MODEL:2 * D_MODEL] * bc


def _bit_transpose32(a):
    rows = [a[j] for j in range(KEY_BITS)]
    j, m = KEY_BITS // 2, 0x0000FFFF
    while j:
        for k in range(KEY_BITS):
            if k & j:
                continue
            t = (rows[k] ^ lax.shift_right_logical(rows[k + j], jnp.int32(j))) & m
            rows[k] = rows[k] ^ t
            rows[k + j] = rows[k + j] ^ lax.shift_left(t, jnp.int32(j))
        j >>= 1
        m = (m ^ (m << j)) & 0xFFFFFFFF
    return jnp.stack(rows)


def _attn_kernel(iqT_ref, iwT_ref, qT_ref, ik_ref, k_ref, vT_ref, *rest, seq_len, topk, n_cast):
    cast_in, (o_ref, *rest) = rest[:n_cast], rest[n_cast:]
    cast_out, scratch = rest[:n_cast], rest[n_cast:]
    sc_ref, planes_ref, bias_ref, l_ref, oT_ref, tlim_ref, ties_ref = scratch
    for src, dst in zip(cast_in, cast_out):
        dst[...] = src[...].astype(BF16)
    assert K_TILE == KEY_BITS * SUBLANES
    max_chunks = seq_len // K_TILE
    i = pl.program_id(1)
    nch = i + 1
    row_k = lax.broadcasted_iota(I32, (K_TILE, Q_TILE), 0)
    lane_q = lax.broadcasted_iota(I32, (K_TILE, Q_TILE), 1)
    rel = row_k - lane_q

    def chunk_start(c):
        return pl.multiple_of(c * K_TILE, K_TILE)

    def chunk_loop(body, init):
        def pair(p, carry):
            return body(2 * p + 1, 1, body(2 * p, 0, carry))
        carry = lax.fori_loop(0, lax.shift_right_logical(nch, 1), pair, init)
        return lax.cond((nch & 1) == 1, lambda cr: body(nch - 1, 0, cr), lambda cr: cr, carry)

    def qk_logits(h, r0):
        return jnp.dot(k_ref[0, h, pl.ds(r0, K_TILE), :],
                       qT_ref[0, h * HEAD_DIM:(h + 1) * HEAD_DIM, :],
                       preferred_element_type=F32)

    def fold_rows(m):
        return jnp.sum(m.reshape(K_TILE // SUBLANES, SUBLANES, Q_TILE), axis=0)

    iw = iwT_ref[0] * ((IDX_HEADS ** -0.5) * (IDX_DIM ** -0.5))

    def score_chunk(c, slot, carry):
        for sb in range(K_TILE // SCORE_ROWS):
            r0 = pl.multiple_of(c * K_TILE + sb * SCORE_ROWS, SCORE_ROWS)
            ikc = ik_ref[0, pl.ds(r0, SCORE_ROWS), :]
            acc = jnp.zeros((SCORE_ROWS, Q_TILE), F32)
            for h in range(IDX_HEADS):
                iq_h = iqT_ref[0, h * IDX_DIM:(h + 1) * IDX_DIM, :]
                d = jnp.dot(ikc, iq_h, preferred_element_type=F32)
                acc = acc + jnp.maximum(d, 0.0) * iw[h:h + 1, :]
            acc = jnp.where(acc == 0.0, 0.0, acc)
            valid = (rel[0:SCORE_ROWS, :] + sb * SCORE_ROWS) <= (i - c) * K_TILE
            sc_ref[pl.ds(r0, SCORE_ROWS), :] = jnp.where(valid, acc, -jnp.inf)
        for h in range(EARLY_QK_HEADS):
            l_ref[h, pl.ds(chunk_start(c), K_TILE), :] = qk_logits(h, chunk_start(c))
        for lt in range(Q_TILE // LANES):
            ls = slice(lt * LANES, (lt + 1) * LANES)
            bits = pltpu.bitcast(sc_ref[pl.ds(chunk_start(c), K_TILE), ls], I32)
            u = bits ^ ((bits >> 31) | INT_MIN)
            planes_ref[c, :, :, ls] = _bit_transpose32(u.reshape(KEY_BITS, SUBLANES, LANES))
        return carry

    @pl.when(jnp.logical_and(pl.program_id(0) == 0, i == 0))
    def _():
        planes_ref[...] = jnp.zeros(planes_ref.shape, I32)

    chunk_loop(score_chunk, 0)

    def count(words):
        pops = [lax.population_count(w) for w in words]
        while len(pops) > 1:
            pops = [a + b for a, b in zip(pops[0::2], pops[1::2])]
        return jnp.sum(pops[0], axis=0, keepdims=True)

    def radix_step(r, carry):
        match, n_gt, thr_u = carry
        ones = [match[c] & planes_ref[c, r] for c in range(max_chunks)]
        n_one = count(ones)
        take = (n_gt + n_one) >= topk
        thr_u = jnp.where(take, thr_u | lax.shift_left(jnp.int32(1), KEY_BITS - 1 - r), thr_u)
        n_gt = jnp.where(take, n_gt, n_gt + n_one)
        match = tuple(jnp.where(take, ones[c], match[c] ^ ones[c]) for c in range(max_chunks))
        return match, n_gt, thr_u

    def radix_threshold():
        match0 = tuple(jnp.broadcast_to(jnp.where(c < nch, -1, 0), (SUBLANES, Q_TILE)).astype(I32)
                       for c in range(max_chunks))
        _, _, thr_u = lax.fori_loop(
            0, KEY_BITS, radix_step,
            (match0, jnp.zeros((1, Q_TILE), I32), jnp.zeros((1, Q_TILE), I32)))
        key = thr_u ^ INT_MIN
        thr = pltpu.bitcast(key ^ ((key >> 31) & 0x7FFFFFFF), F32)
        return jnp.where(thr > -jnp.inf, thr, -jnp.inf)

    def rank_counts(t):
        def body(c, slot, acc):
            sc = sc_ref[pl.ds(chunk_start(c), K_TILE), :]
            code = jnp.where(sc > t, 1 << 16, jnp.where(sc == t, 1, 0)).astype(I32)
            return acc + fold_rows(code)
        both = jnp.sum(chunk_loop(body, jnp.zeros((SUBLANES, Q_TILE), I32)),
                       axis=0, keepdims=True)
        gt = both >> 16
        return gt, gt + (both & 0xFFFF)

    def off_by(state):
        _, gt, ge, steps = state
        bad = jnp.logical_or(gt >= topk, ge < topk)
        return jnp.logical_and(jnp.max(jnp.where(bad, 1, 0)) > 0, steps < topk)

    def walk(state):
        t, gt, ge, steps = state

        lower = lambda a, b: jnp.where(a < b, a, b)
        upper = lambda a, b: jnp.where(a > b, a, b)

        def fold(op, x):
            while x.shape[0] > 1:
                half = x.shape[0] // 2
                x = op(x[:half], x[half:])
            return x

        def body(c, slot, carry):
            sc = sc_ref[pl.ds(chunk_start(c), K_TILE), :]
            return (lower(carry[0], fold(lower, jnp.where(sc > t, sc, jnp.inf))),
                    upper(carry[1], fold(upper, jnp.where(sc < t, sc, -jnp.inf))))
        up, dn = chunk_loop(body, (jnp.full((1, Q_TILE), jnp.inf, F32),
                                   jnp.full((1, Q_TILE), -jnp.inf, F32)))
        t = jnp.where(gt >= topk, up, jnp.where(ge < topk, dn, t))
        return (t,) + rank_counts(t) + (steps + 1,)

    def checked(t):
        return lax.while_loop(off_by, walk, (t,) + rank_counts(t) + (jnp.int32(0),))[:3]

    none = jnp.zeros((1, Q_TILE), I32)
    thr, n_gt, n_ge = lax.cond(
        (i + 1) * Q_TILE > topk, lambda: checked(radix_threshold()),
        lambda: (jnp.full((1, Q_TILE), -jnp.inf, F32), none, none))
    n_eq = n_ge - n_gt
    need = topk - n_gt
    has_k = thr > -jnp.inf
    thr = jnp.where(has_k, thr, jnp.finfo(F32).min)

    tlim_ref[...] = jnp.full((SUBLANES, Q_TILE), seq_len, I32)

    @pl.when(jnp.max(jnp.where(has_k, n_eq - need, 0)) > 0)
    def _():
        sub = lax.broadcasted_iota(I32, (SUBLANES, Q_TILE), 0)

        ties_ref[...] = jnp.zeros(ties_ref.shape, I32)

        def pack_chunk(c, slot, carry):
            eq = (sc_ref[pl.ds(chunk_start(c), K_TILE), :] == thr).reshape(
                KEY_BITS, SUBLANES, Q_TILE)
            word = jnp.zeros((SUBLANES, Q_TILE), I32)
            for g in range(KEY_BITS):
                word = word | jnp.where(eq[g], INT_MIN if g == 0 else 1 << (KEY_BITS - 1 - g), 0)
            ties_ref[c] = word
            return carry

        chunk_loop(pack_chunk, 0)
        ties = [ties_ref[c] for c in range(max_chunks)]

        def ties_before(lim):
            tot = jnp.zeros((SUBLANES, Q_TILE), I32)
            for c in range(max_chunks):
                g_last = (lim - (c * K_TILE + 1) - sub) >> 3
                below = jnp.where(
                    g_last < 0, 0,
                    lax.shift_left(jnp.int32(-1),
                                   KEY_BITS - 1 - jnp.minimum(g_last, KEY_BITS - 1)))
                tot = tot + lax.population_count(ties[c] & below)
            return jnp.sum(tot, axis=0, keepdims=True)

        def tie_step(t, lim):
            bit = lax.shift_left(jnp.int32(1), (seq_len.bit_length() - 2) - t)
            cand = lim | bit
            return jnp.where(ties_before(cand) < need, cand, lim)
        lim = lax.fori_loop(0, seq_len.bit_length() - 1, tie_step,
                            jnp.zeros((1, Q_TILE), I32))
        tlim_ref[...] = jnp.broadcast_to(jnp.where(n_eq > need, lim, seq_len),
                                         (SUBLANES, Q_TILE))

    tlim = tlim_ref[0:1, :]

    def logits_chunk(c, slot, m8):
        r0 = chunk_start(c)
        sc = sc_ref[pl.ds(r0, K_TILE), :]
        tie_ok = jnp.logical_and(sc == thr, (row_k + r0) <= tlim)
        sel = jnp.logical_or(sc > thr, tie_ok)
        bias_ref[slot] = jnp.where(sel, 0.0, -jnp.inf).astype(F32)
        out = []
        for h in range(N_HEADS):
            raw = l_ref[h, pl.ds(r0, K_TILE), :] if h < EARLY_QK_HEADS else qk_logits(h, r0)
            l = raw + bias_ref[slot]
            l_ref[h, pl.ds(r0, K_TILE), :] = l
            out.append(jnp.maximum(
                m8[h], jnp.max(l.reshape(K_TILE // SUBLANES, SUBLANES, Q_TILE), axis=0)))
        return tuple(out)

    m8 = chunk_loop(
        logits_chunk,
        tuple(jnp.full((SUBLANES, Q_TILE), -jnp.inf, F32) for _ in range(N_HEADS)))
    m_all = [jnp.max(m, axis=0, keepdims=True) for m in m8]

    oT_ref[...] = jnp.zeros(oT_ref.shape, F32)

    def pv_chunk(c, slot, s8):
        r0 = chunk_start(c)
        out = []
        for h in range(N_HEADS):
            hs = slice(h * HEAD_DIM, (h + 1) * HEAD_DIM)
            p = jnp.exp2(l_ref[h, pl.ds(r0, K_TILE), :] - m_all[h])
            out.append(s8[h] + fold_rows(p))
            oT_ref[hs, :] += jnp.dot(vT_ref[0, hs, pl.ds(r0, K_TILE)], p.astype(BF16),
                                     preferred_element_type=F32)
        return tuple(out)

    s8 = chunk_loop(
        pv_chunk, tuple(jnp.zeros((SUBLANES, Q_TILE), F32) for _ in range(N_HEADS)))
    for h in range(N_HEADS):
        hs = slice(h * HEAD_DIM, (h + 1) * HEAD_DIM)
        oT_ref[hs, :] = oT_ref[hs, :] / jnp.sum(s8[h], axis=0, keepdims=True)

    o_ref[0] = oT_ref[...].T.astype(BF16)


def _ffn_kernel(x_ref, attn_ref, ga_ref, mc_ref, wao_ref, wo_ref, g2_ref,
                wup_ref, fcw_ref, wdn_ref, gf_ref, out_ref,
                gbuf_ref, vbuf_ref, h1_ref, xn2_ref, merged_ref, *, tiles_per_seq):
    t = pl.program_id(0)
    tm = x_ref.shape[1]
    cur = lax.rem(t + 1, 2)
    nxt = lax.rem(t, 2)

    @pl.when(lax.rem(t - 1, tiles_per_seq) == 0)
    def _():
        zeros = jnp.zeros((SUBLANES, FFN_CHUNK), F32)
        for n in range(D_FF // FFN_CHUNK):
            gbuf_ref[n, 0:SUBLANES, :] = zeros
            vbuf_ref[n, 0:SUBLANES, :] = zeros

    def merge_next():
        ba = jnp.dot(attn_ref[0], wao_ref[...], preferred_element_type=F32)
        return (ga_ref[0] * ba + mc_ref[0]).astype(BF16)

    def norm_next(merged):
        h1n = x_ref[0] + jnp.dot(merged, wo_ref[...], preferred_element_type=F32)
        h1_ref[nxt] = h1n
        xn2_ref[nxt] = _rms(h1n, g2_ref[...]).astype(BF16)

    @pl.when(t == 0)
    def _():
        norm_next(merge_next())

    @pl.when(t > 0)
    def _():
        xn2 = xn2_ref[cur]

        def causal_conv(u, buf_ref, col0):
            buf_ref[SUBLANES:SUBLANES + tm, :] = u
            u1 = buf_ref[SUBLANES - 1:SUBLANES - 1 + tm, :]
            u2 = buf_ref[SUBLANES - 2:SUBLANES - 2 + tm, :]
            buf_ref[0:SUBLANES, :] = u[tm - SUBLANES:, :]
            w = fcw_ref[:, col0:col0 + FFN_CHUNK]
            return u2 * w[0:1, :] + u1 * w[1:2, :] + u * w[2:3, :]

        def up_proj(n):
            c0 = n * FFN_CHUNK
            ug = jnp.dot(xn2, wup_ref[:, c0:c0 + FFN_CHUNK], preferred_element_type=F32)
            uv = jnp.dot(xn2, wup_ref[:, D_FF + c0:D_FF + c0 + FFN_CHUNK],
                         preferred_element_type=F32)
            return ug, uv

        nchunk = D_FF // FFN_CHUNK
        out_ref[0] = h1_ref[cur]
        ahead = [up_proj(n) for n in range(min(FFN_LOOKAHEAD, nchunk))]
        acts = []
        for n in range(nchunk):
            c0 = n * FFN_CHUNK
            ug, uv = ahead.pop(0)
            if n + FFN_LOOKAHEAD < nchunk:
                ahead.append(up_proj(n + FFN_LOOKAHEAD))
            if n == MERGE_AT_CHUNK:
                merged_ref[...] = merge_next()
            if n == NORM_AT_CHUNK:
                norm_next(merged_ref[...])
            gt = causal_conv(ug, gbuf_ref.at[n], c0)
            val = causal_conv(uv, vbuf_ref.at[n], D_FF + c0)
            acts.append((gt * jax.nn.sigmoid(gt) * val).astype(BF16))
            if len(acts) == DOWN_GROUP or n == nchunk - 1:
                r0 = (n + 1 - len(acts)) * FFN_CHUNK
                act = acts[0] if len(acts) == 1 else jnp.concatenate(acts, axis=1)
                out_ref[0] += jnp.dot(act, wdn_ref[r0:r0 + act.shape[1], :],
                                      preferred_element_type=F32)
                acts = []

        if NORM_AT_CHUNK >= nchunk:
            norm_next(merged_ref[...])
        out_ref[0] = _rms(out_ref[0], gf_ref[...])


def _sparse_attention(iqT, iwT, qT, ik, k, vT, *, topk, cast_along=()):
    B, _, S = qT.shape
    nq = S // Q_TILE
    featT = lambda c: pl.BlockSpec((1, c, Q_TILE), lambda b, j: (b, 0, j))

    def rows_spec(w):
        rows = next(r for r in range(BF16_ROWS, w.shape[0] + 1, BF16_ROWS)
                    if w.shape[0] % r == 0 and w.shape[0] // r <= B * nq)
        last = w.shape[0] // rows - 1
        return pl.BlockSpec((rows, w.shape[1]), lambda b, j: (jnp.minimum(b * nq + j, last), 0))

    cast_specs = [rows_spec(w) for w in cast_along]
    outs = pl.pallas_call(
        functools.partial(_attn_kernel, seq_len=S, topk=topk, n_cast=len(cast_along)),
        grid=(B, nq),
        in_specs=[featT(IDX_HEADS * IDX_DIM), featT(IDX_HEADS), featT(ATTN_WIDTH),
                  pl.BlockSpec((1, S, IDX_DIM), lambda b, j: (b, 0, 0)),
                  pl.BlockSpec((1, N_HEADS, S, HEAD_DIM), lambda b, j: (b, 0, 0, 0)),
                  pl.BlockSpec((1, ATTN_WIDTH, S), lambda b, j: (b, 0, 0))] + cast_specs,
        out_specs=[pl.BlockSpec((1, Q_TILE, ATTN_WIDTH), lambda b, j: (b, j, 0))] + cast_specs,
        out_shape=[jax.ShapeDtypeStruct((B, S, ATTN_WIDTH), BF16)]
        + [jax.ShapeDtypeStruct(w.shape, BF16) for w in cast_along],
        scratch_shapes=[pltpu.VMEM((S, Q_TILE), F32),
                        pltpu.VMEM((S // K_TILE, KEY_BITS, SUBLANES, Q_TILE), I32),
                        pltpu.VMEM((2, K_TILE, Q_TILE), F32),
                        pltpu.VMEM((N_HEADS, S, Q_TILE), F32),
                        pltpu.VMEM((ATTN_WIDTH, Q_TILE), F32),
                        pltpu.VMEM((SUBLANES, Q_TILE), I32),
                        pltpu.VMEM((S // K_TILE, SUBLANES, Q_TILE), I32)],
        compiler_params=pltpu.CompilerParams(
            dimension_semantics=("arbitrary", "arbitrary"), vmem_limit_bytes=VMEM_LIMIT),
        name="attn",
    )(iqT, iwT, qT, ik, k, vT, *cast_along)
    return outs[0], tuple(outs[1:])


def _const_spec(shape):
    nd = len(shape)
    return pl.BlockSpec(shape, lambda *_: (0,) * nd, pipeline_mode=pl.Buffered(1))


def kernel(x, norm_mix_g, w_in, b_gate, w_attn_out, conv_w, w_conv_out, w_o,
           norm_ffn_g, w_up, ffn_conv_w, w_down, norm_final_g):
    B, S, D = x.shape
    assert D == D_MODEL and S % PROJ_TILE == 0 and S % FFN_TILE == 0
    assert S % Q_TILE == 0 and Q_TILE == K_TILE
    assert norm_mix_g.shape[0] == 1, "single-layer block"
    topk = min(INDEX_TOPK_MAX, S // 4)
    tm = PROJ_TILE

    w = w_in[0]
    o_q, o_k, o_v, o_iq = 0, ATTN_WIDTH, 2 * ATTN_WIDTH, 3 * ATTN_WIDTH
    o_ik = o_iq + IDX_HEADS * IDX_DIM
    o_iw = o_ik + IDX_DIM
    o_ch = o_iw + IDX_HEADS
    o_g = o_ch + 3 * CONV_WIDTH
    wqT = w[:, o_q:o_k].T.astype(BF16)
    wvT = w[:, o_v:o_iq].T.astype(BF16)
    wiqT = w[:, o_iq:o_ik].T.astype(BF16)
    wiwT = jnp.pad(w[:, o_iw:o_ch].T, ((0, BF16_ROWS - IDX_HEADS), (0, 0))).astype(BF16)
    wk = w[:, o_k:o_v].astype(BF16)
    wik = jnp.pad(w[:, o_ik:o_iw], ((0, 0), (0, LANES - IDX_DIM))).astype(BF16)
    wconv = w[:, o_ch:o_g].astype(BF16)
    wg = w[:, o_g:].astype(BF16)
    g1 = norm_mix_g.reshape(1, D)
    bg = b_gate.reshape(1, 2 * D)
    cw = conv_w[0]
    wco = w_conv_out[0].astype(BF16)

    cparams = pltpu.CompilerParams(
        dimension_semantics=("arbitrary", "arbitrary"), vmem_limit_bytes=VMEM_LIMIT)

    tok = lambda c, t: pl.BlockSpec((1, t, c), lambda b, j: (b, j, 0))
    featT = lambda c, t: pl.BlockSpec((1, c, t), lambda b, j: (b, 0, j))

    k, ik, qT, vT, iqT, iwT, ga, mc = pl.pallas_call(
        _proj_kernel,
        grid=(B, S // tm),
        in_specs=[tok(D, tm), _const_spec((1, D)),
                  _const_spec(wk.shape), _const_spec(wik.shape),
                  _const_spec(wconv.shape), _const_spec(wg.shape),
                  _const_spec(wqT.shape), _const_spec(wvT.shape),
                  _const_spec(wiqT.shape), _const_spec(wiwT.shape),
                  _const_spec(bg.shape), _const_spec(cw.shape), _const_spec(wco.shape)],
        out_specs=[pl.BlockSpec((1, N_HEADS, tm, HEAD_DIM), lambda b, j: (b, 0, j, 0)),
                   tok(IDX_DIM, tm), featT(ATTN_WIDTH, tm), featT(ATTN_WIDTH, tm),
                   featT(IDX_HEADS * IDX_DIM, tm), featT(IDX_HEADS, tm),
                   tok(D, tm), tok(D, tm)],
        out_shape=[jax.ShapeDtypeStruct((B, N_HEADS, S, HEAD_DIM), BF16),
                   jax.ShapeDtypeStruct((B, S, IDX_DIM), BF16),
                   jax.ShapeDtypeStruct((B, ATTN_WIDTH, S), BF16),
                   jax.ShapeDtypeStruct((B, ATTN_WIDTH, S), BF16),
                   jax.ShapeDtypeStruct((B, IDX_HEADS * IDX_DIM, S), BF16),
                   jax.ShapeDtypeStruct((B, IDX_HEADS, S), F32),
                   jax.ShapeDtypeStruct((B, S, D), F32),
                   jax.ShapeDtypeStruct((B, S, D), F32)],
        scratch_shapes=[pltpu.VMEM((tm + SUBLANES, CONV_WIDTH), F32)],
        compiler_params=cparams,
        name="proj",
    )(x, g1, wk, wik, wconv, wg, wqT, wvT, wiqT, wiwT, bg, cw, wco)

    attn, (wao, wo, wup, wdn) = _sparse_attention(
        iqT, iwT, qT, ik, k, vT, topk=topk,
        cast_along=(w_attn_out[0], w_o[0], w_up[0], w_down[0]))
    fcw = ffn_conv_w[0]
    g2 = norm_ffn_g.reshape(1, D)
    gf = norm_final_g.reshape(1, D)
    nchunk = D_FF // FFN_CHUNK
    assert nchunk * FFN_CHUNK == D_FF
    tf = FFN_TILE

    nt = S // tf
    n_tiles = B * nt

    def tile_in(c):
        def index(t):
            tt = jnp.minimum(t, n_tiles - 1)
            return (tt // nt, tt % nt, 0)
        return pl.BlockSpec((1, tf, c), index)

    def tile_out(t):
        tt = jnp.maximum(t - 1, 0)
        return (tt // nt, tt % nt, 0)

    out = pl.pallas_call(
        functools.partial(_ffn_kernel, tiles_per_seq=nt),
        grid=(n_tiles + 1,),
        in_specs=[tile_in(D), tile_in(ATTN_WIDTH), tile_in(D), tile_in(D),
                  _const_spec(wao.shape), _const_spec(wo.shape), _const_spec(g2.shape),
                  _const_spec(wup.shape), _const_spec(fcw.shape), _const_spec(wdn.shape),
                  _const_spec(gf.shape)],
        out_specs=pl.BlockSpec((1, tf, D), tile_out),
        out_shape=jax.ShapeDtypeStruct((B, S, D), F32),
        scratch_shapes=[pltpu.VMEM((nchunk, tf + SUBLANES, FFN_CHUNK), F32),
                        pltpu.VMEM((nchunk, tf + SUBLANES, FFN_CHUNK), F32),
                        pltpu.VMEM((2, tf, D), F32),
                        pltpu.VMEM((2, tf, D), BF16),
                        pltpu.VMEM((tf, D), BF16)],
        compiler_params=pltpu.CompilerParams(
            dimension_semantics=("arbitrary",), vmem_limit_bytes=VMEM_LIMIT),
        name="ffn",
    )(x, attn, ga, mc, wao, wo, g2, wup, fcw, wdn, gf)
    return out
```

```python
import functools

import jax
import jax.numpy as jnp
from jax import lax
from jax.experimental import pallas as pl
from jax.experimental.pallas import tpu as pltpu

D_MODEL = 1024
N_HEADS = 8
HEAD_DIM = 64
ATTN_WIDTH = N_HEADS * HEAD_DIM
IDX_HEADS = 8
IDX_DIM = 64
INDEX_TOPK_MAX = 256
CONV_WIDTH = 512
D_FF = 2816
EPS = 1e-6
LOG2E = 1.4426950408889634

F32 = jnp.float32
BF16 = jnp.bfloat16
I32 = jnp.int32
KEY_BITS = 32
INT_MIN = -(2 ** 31)

SUBLANES = 8
BF16_ROWS = 16
LANES = 128
MXU_DIM = 256
VMEM_LIMIT = 56 * 1024 * 1024

PROJ_TILE = 512
FFN_TILE = 256
Q_TILE = MXU_DIM
K_TILE = 256
SCORE_ROWS = 64
EARLY_QK_HEADS = 4
FFN_CHUNK = 256
FFN_LOOKAHEAD = 2
DOWN_GROUP = 4
MERGE_AT_CHUNK = 8
NORM_AT_CHUNK = 10

_NT = (((1,), (1,)), ((), ()))


def _rms(x, g):
    return x * lax.rsqrt(jnp.mean(x * x, axis=-1, keepdims=True) + EPS) * g


def _shifted_rows(buf_ref, u, first_tile):
    tm = u.shape[0]

    @pl.when(first_tile)
    def _():
        buf_ref[0:SUBLANES, :] = jnp.zeros((SUBLANES, u.shape[1]), F32)

    buf_ref[SUBLANES:SUBLANES + tm, :] = u
    u1 = buf_ref[SUBLANES - 1:SUBLANES - 1 + tm, :]
    u2 = buf_ref[SUBLANES - 2:SUBLANES - 2 + tm, :]
    buf_ref[0:SUBLANES, :] = u[tm - SUBLANES:, :]
    return u1, u2


def _proj_kernel(x_ref, g_ref, wk_ref, wik_ref, wconv_ref, wg_ref,
                 wqT_ref, wvT_ref, wiqT_ref, wiwT_ref,
                 bg_ref, cw_ref, wco_ref,
                 k_ref, ik_ref, qT_ref, vT_ref, iqT_ref, iwT_ref, ga_ref, mc_ref,
                 ubuf_ref):
    j = pl.program_id(1)
    xn = _rms(x_ref[0], g_ref[...]).astype(BF16)

    pc = jnp.dot(xn, wconv_ref[...], preferred_element_type=F32)
    ch = pc[:, 0:CONV_WIDTH]
    cb = pc[:, CONV_WIDTH:2 * CONV_WIDTH]
    cc = pc[:, 2 * CONV_WIDTH:3 * CONV_WIDTH]
    u = cc * ch
    u1, u2 = _shifted_rows(ubuf_ref, u, j == 0)
    cw = cw_ref[...]
    conv = u2 * cw[0:1, :] + u1 * cw[1:2, :] + u * cw[2:3, :]
    cbconv = (cb * conv).astype(BF16)

    g = jax.nn.sigmoid(jnp.dot(xn, wg_ref[...], preferred_element_type=F32)
                       + bg_ref[...])
    ga_ref[0] = g[:, 0:D_MODEL]

    qT = lax.dot_general(wqT_ref[...], xn, _NT, preferred_element_type=F32)
    qT_ref[0] = (qT * (HEAD_DIM ** -0.5 * LOG2E)).astype(BF16)
    vT = lax.dot_general(wvT_ref[...], xn, _NT, preferred_element_type=F32)
    vT_ref[0] = vT.astype(BF16)
    iqT = lax.dot_general(wiqT_ref[...], xn, _NT, preferred_element_type=F32)
    iqT_ref[0] = iqT.astype(BF16)
    iwT = lax.dot_general(wiwT_ref[...], xn, _NT, preferred_element_type=F32)
    iwT_ref[0] = iwT[0:IDX_HEADS, :]

    kk = jnp.dot(xn, wk_ref[...], preferred_element_type=F32)
    for h in range(N_HEADS):
        k_ref[0, h] = kk[:, h * HEAD_DIM:(h + 1) * HEAD_DIM].astype(BF16)
    ik = jnp.dot(xn, wik_ref[...], preferred_element_type=F32)
    ik_ref[0] = ik[:, 0:IDX_DIM].astype(BF16)

    bc = jnp.dot(cbconv, wco_ref[...], preferred_element_type=F32)
    mc_ref[0] = g[:, D_MODEL:2 * D_MODEL] * bc


def _bit_transpose32(a):
    rows = [a[j] for j in range(KEY_BITS)]
    j, m = KEY_BITS // 2, 0x0000FFFF
    while j:
        for k in range(KEY_BITS):
            if k & j:
                continue
            t = (rows[k] ^ lax.shift_right_logical(rows[k + j], jnp.int32(j))) & m
            rows[k] = rows[k] ^ t
            rows[k + j] = rows[k + j] ^ lax.shift_left(t, jnp.int32(j))
        j >>= 1
        m = (m ^ (m << j)) & 0xFFFFFFFF
    return jnp.stack(rows)


def _attn_kernel(iqT_ref, iwT_ref, qT_ref, ik_ref, k_ref, vT_ref, *rest, seq_len, topk, n_cast):
    cast_in, (o_ref, *rest) = rest[:n_cast], rest[n_cast:]
    cast_out, scratch = rest[:n_cast], rest[n_cast:]
    sc_ref, planes_ref, bias_ref, l_ref, oT_ref, tlim_ref, ties_ref = scratch
    for src, dst in zip(cast_in, cast_out):
        dst[...] = src[...].astype(BF16)
    assert K_TILE == KEY_BITS * SUBLANES
    max_chunks = seq_len // K_TILE
    i = pl.program_id(1)
    nch = i + 1
    row_k = lax.broadcasted_iota(I32, (K_TILE, Q_TILE), 0)
    lane_q = lax.broadcasted_iota(I32, (K_TILE, Q_TILE), 1)
    rel = row_k - lane_q

    def chunk_start(c):
        return pl.multiple_of(c * K_TILE, K_TILE)

    def chunk_loop(body, init):
        def pair(p, carry):
            return body(2 * p + 1, 1, body(2 * p, 0, carry))
        carry = lax.fori_loop(0, lax.shift_right_logical(nch, 1), pair, init)
        return lax.cond((nch & 1) == 1, lambda cr: body(nch - 1, 0, cr), lambda cr: cr, carry)

    def qk_logits(h, r0):
        return jnp.dot(k_ref[0, h, pl.ds(r0, K_TILE), :],
                       qT_ref[0, h * HEAD_DIM:(h + 1) * HEAD_DIM, :],
                       preferred_element_type=F32)

    def fold_rows(m):
        return jnp.sum(m.reshape(K_TILE // SUBLANES, SUBLANES, Q_TILE), axis=0)

    iw = iwT_ref[0] * ((IDX_HEADS ** -0.5) * (IDX_DIM ** -0.5))

    def score_chunk(c, slot, carry):
        for sb in range(K_TILE // SCORE_ROWS):
            r0 = pl.multiple_of(c * K_TILE + sb * SCORE_ROWS, SCORE_ROWS)
            ikc = ik_ref[0, pl.ds(r0, SCORE_ROWS), :]
            acc = jnp.zeros((SCORE_ROWS, Q_TILE), F32)
            for h in range(IDX_HEADS):
                iq_h = iqT_ref[0, h * IDX_DIM:(h + 1) * IDX_DIM, :]
                d = jnp.dot(ikc, iq_h, preferred_element_type=F32)
                acc = acc + jnp.maximum(d, 0.0) * iw[h:h + 1, :]
            acc = jnp.where(acc == 0.0, 0.0, acc)
            valid = (rel[0:SCORE_ROWS, :] + sb * SCORE_ROWS) <= (i - c) * K_TILE
            sc_ref[pl.ds(r0, SCORE_ROWS), :] = jnp.where(valid, acc, -jnp.inf)
        for h in range(EARLY_QK_HEADS):
            l_ref[h, pl.ds(chunk_start(c), K_TILE), :] = qk_logits(h, chunk_start(c))
        for lt in range(Q_TILE // LANES):
            ls = slice(lt * LANES, (lt + 1) * LANES)
            bits = pltpu.bitcast(sc_ref[pl.ds(chunk_start(c), K_TILE), ls], I32)
            u = bits ^ ((bits >> 31) | INT_MIN)
            planes_ref[c, :, :, ls] = _bit_transpose32(u.reshape(KEY_BITS, SUBLANES, LANES))
        return carry

    @pl.when(jnp.logical_and(pl.program_id(0) == 0, i == 0))
    def _():
        planes_ref[...] = jnp.zeros(planes_ref.shape, I32)

    chunk_loop(score_chunk, 0)

    def count(words):
        pops = [lax.population_count(w) for w in words]
        while len(pops) > 1:
            pops = [a + b for a, b in zip(pops[0::2], pops[1::2])]
        return jnp.sum(pops[0], axis=0, keepdims=True)

    def radix_step(r, carry):
        match, n_gt, thr_u = carry
        ones = [match[c] & planes_ref[c, r] for c in range(max_chunks)]
        n_one = count(ones)
        take = (n_gt + n_one) >= topk
        thr_u = jnp.where(take, thr_u | lax.shift_left(jnp.int32(1), KEY_BITS - 1 - r), thr_u)
        n_gt = jnp.where(take, n_gt, n_gt + n_one)
        match = tuple(jnp.where(take, ones[c], match[c] ^ ones[c]) for c in range(max_chunks))
        return match, n_gt, thr_u

    def radix_threshold():
        match0 = tuple(jnp.broadcast_to(jnp.where(c < nch, -1, 0), (SUBLANES, Q_TILE)).astype(I32)
                       for c in range(max_chunks))
        _, _, thr_u = lax.fori_loop(
            0, KEY_BITS, radix_step,
            (match0, jnp.zeros((1, Q_TILE), I32), jnp.zeros((1, Q_TILE), I32)))
        key = thr_u ^ INT_MIN
        thr = pltpu.bitcast(key ^ ((key >> 31) & 0x7FFFFFFF), F32)
        return jnp.where(thr > -jnp.inf, thr, -jnp.inf)

    def rank_counts(t):
        def body(c, slot, acc):
            sc = sc_ref[pl.ds(chunk_start(c), K_TILE), :]
            code = jnp.where(sc > t, 1 << 16, jnp.where(sc == t, 1, 0)).astype(I32)
            return acc + fold_rows(code)
        both = jnp.sum(chunk_loop(body, jnp.zeros((SUBLANES, Q_TILE), I32)),
                       axis=0, keepdims=True)
        gt = both >> 16
        return gt, gt + (both & 0xFFFF)

    def off_by(state):
        _, gt, ge, steps = state
        bad = jnp.logical_or(gt >= topk, ge < topk)
        return jnp.logical_and(jnp.max(jnp.where(bad, 1, 0)) > 0, steps < topk)

    def walk(state):
        t, gt, ge, steps = state

        lower = lambda a, b: jnp.where(a < b, a, b)
        upper = lambda a, b: jnp.where(a > b, a, b)

        def fold(op, x):
            while x.shape[0] > 1:
                half = x.shape[0] // 2
                x = op(x[:half], x[half:])
            return x

        def body(c, slot, carry):
            sc = sc_ref[pl.ds(chunk_start(c), K_TILE), :]
            return (lower(carry[0], fold(lower, jnp.where(sc > t, sc, jnp.inf))),
                    upper(carry[1], fold(upper, jnp.where(sc < t, sc, -jnp.inf))))
        up, dn = chunk_loop(body, (jnp.full((1, Q_TILE), jnp.inf, F32),
                                   jnp.full((1, Q_TILE), -jnp.inf, F32)))
        t = jnp.where(gt >= topk, up, jnp.where(ge < topk, dn, t))
        return (t,) + rank_counts(t) + (steps + 1,)

    def attend(thr, tlim, staged_heads, tile_flag):
        def logits_chunk(c, slot, m8):
            r0 = chunk_start(c)
            sc = sc_ref[pl.ds(r0, K_TILE), :]
            tie_ok = jnp.logical_and(sc == thr, (row_k + r0) <= tlim)
            sel = jnp.logical_or(sc > thr, tie_ok)
            bias_ref[slot] = jnp.where(sel, 0.0, -jnp.inf).astype(F32)
            out = []
            for h in range(N_HEADS):
                raw = l_ref[h, pl.ds(r0, K_TILE), :] if h < staged_heads else qk_logits(h, r0)
                l = raw + bias_ref[slot]
                l_ref[h, pl.ds(r0, K_TILE), :] = l
                out.append(jnp.maximum(
                    m8[h], jnp.max(l.reshape(K_TILE // SUBLANES, SUBLANES, Q_TILE), axis=0)))
            return tuple(out)

        m8 = chunk_loop(
            logits_chunk,
            tuple(jnp.full((SUBLANES, Q_TILE), -jnp.inf, F32) for _ in range(N_HEADS)))
        m_all = [jnp.max(m, axis=0, keepdims=True) for m in m8]

        oT_ref[...] = jnp.zeros(oT_ref.shape, F32)

        def pv_chunk(c, slot, s8):
            r0 = chunk_start(c)
            out = []
            for h in range(N_HEADS):
                hs = slice(h * HEAD_DIM, (h + 1) * HEAD_DIM)
                p = jnp.exp2(l_ref[h, pl.ds(r0, K_TILE), :] - m_all[h])
                out.append(s8[h] + fold_rows(p))
                oT_ref[hs, :] += jnp.dot(vT_ref[0, hs, pl.ds(r0, K_TILE)], p.astype(BF16),
                                         preferred_element_type=F32)
            return tuple(out)

        s8 = chunk_loop(
            pv_chunk, tuple(jnp.zeros((SUBLANES, Q_TILE), F32) for _ in range(N_HEADS)))
        flag = tile_flag()
        for h in range(N_HEADS):
            hs = slice(h * HEAD_DIM, (h + 1) * HEAD_DIM)
            oT_ref[hs, :] = oT_ref[hs, :] / jnp.sum(s8[h], axis=0, keepdims=True)
        o_ref[0] = oT_ref[...].T.astype(BF16)
        return flag

    thr0, gt0, ge0 = lax.cond(
        (i + 1) * Q_TILE > topk, lambda: (lambda t: (t,) + rank_counts(t))(radix_threshold()),
        lambda: (jnp.full((1, Q_TILE), -jnp.inf, F32), jnp.zeros((1, Q_TILE), I32),
                 jnp.full((1, Q_TILE), topk, I32)))
    floored = lambda t: jnp.where(t > -jnp.inf, t, jnp.finfo(F32).min)

    redo_vec = jnp.logical_or(jnp.logical_or(gt0 >= topk, ge0 < topk),
                              jnp.logical_and(thr0 > -jnp.inf, ge0 > topk))
    redo = attend(floored(thr0), jnp.full((1, Q_TILE), seq_len, I32), EARLY_QK_HEADS,
                  lambda: jnp.max(jnp.where(redo_vec, 1, 0)))

    @pl.when(redo > 0)
    def _():
        thr, n_gt, n_ge, _ = lax.while_loop(off_by, walk, (thr0, gt0, ge0, jnp.int32(0)))
        n_eq = n_ge - n_gt
        need = topk - n_gt
        has_k = thr > -jnp.inf
        thr = floored(thr)

        tlim_ref[...] = jnp.full((SUBLANES, Q_TILE), seq_len, I32)

        @pl.when(jnp.max(jnp.where(has_k, n_eq - need, 0)) > 0)
        def _():
            sub = lax.broadcasted_iota(I32, (SUBLANES, Q_TILE), 0)

            ties_ref[...] = jnp.zeros(ties_ref.shape, I32)

            def pack_chunk(c, slot, carry):
                eq = (sc_ref[pl.ds(chunk_start(c), K_TILE), :] == thr).reshape(
                    KEY_BITS, SUBLANES, Q_TILE)
                word = jnp.zeros((SUBLANES, Q_TILE), I32)
                for g in range(KEY_BITS):
                    word = word | jnp.where(
                        eq[g], INT_MIN if g == 0 else 1 << (KEY_BITS - 1 - g), 0)
                ties_ref[c] = word
                return carry

            chunk_loop(pack_chunk, 0)
            ties = [ties_ref[c] for c in range(max_chunks)]

            def ties_before(lim):
                tot = jnp.zeros((SUBLANES, Q_TILE), I32)
                for c in range(max_chunks):
                    g_last = (lim - (c * K_TILE + 1) - sub) >> 3
                    below = jnp.where(
                        g_last < 0, 0,
                        lax.shift_left(jnp.int32(-1),
                                       KEY_BITS - 1 - jnp.minimum(g_last, KEY_BITS - 1)))
                    tot = tot + lax.population_count(ties[c] & below)
                return jnp.sum(tot, axis=0, keepdims=True)

            def tie_step(t, lim):
                bit = lax.shift_left(jnp.int32(1), (seq_len.bit_length() - 2) - t)
                cand = lim | bit
                return jnp.where(ties_before(cand) < need, cand, lim)
            lim = lax.fori_loop(0, seq_len.bit_length() - 1, tie_step,
                                jnp.zeros((1, Q_TILE), I32))
            tlim_ref[...] = jnp.broadcast_to(jnp.where(n_eq > need, lim, seq_len),
                                             (SUBLANES, Q_TILE))

        attend(thr, tlim_ref[0:1, :], 0, lambda: None)


def _ffn_kernel(x_ref, attn_ref, ga_ref, mc_ref, wao_ref, wo_ref, g2_ref,
                wup_ref, fcw_ref, wdn_ref, gf_ref, out_ref,
                gbuf_ref, vbuf_ref, h1_ref, xn2_ref, merged_ref, *, tiles_per_seq):
    t = pl.program_id(0)
    tm = x_ref.shape[1]
    cur = lax.rem(t + 1, 2)
    nxt = lax.rem(t, 2)

    @pl.when(lax.rem(t - 1, tiles_per_seq) == 0)
    def _():
        zeros = jnp.zeros((SUBLANES, FFN_CHUNK), F32)
        for n in range(D_FF // FFN_CHUNK):
            gbuf_ref[n, 0:SUBLANES, :] = zeros
            vbuf_ref[n, 0:SUBLANES, :] = zeros

    def merge_next():
        ba = jnp.dot(attn_ref[0], wao_ref[...], preferred_element_type=F32)
        return (ga_ref[0] * ba + mc_ref[0]).astype(BF16)

    def norm_next(merged):
        h1n = x_ref[0] + jnp.dot(merged, wo_ref[...], preferred_element_type=F32)
        h1_ref[nxt] = h1n
        xn2_ref[nxt] = _rms(h1n, g2_ref[...]).astype(BF16)

    @pl.when(t == 0)
    def _():
        norm_next(merge_next())

    @pl.when(t > 0)
    def _():
        xn2 = xn2_ref[cur]

        def causal_conv(u, buf_ref, col0):
            buf_ref[SUBLANES:SUBLANES + tm, :] = u
            u1 = buf_ref[SUBLANES - 1:SUBLANES - 1 + tm, :]
            u2 = buf_ref[SUBLANES - 2:SUBLANES - 2 + tm, :]
            buf_ref[0:SUBLANES, :] = u[tm - SUBLANES:, :]
            w = fcw_ref[:, col0:col0 + FFN_CHUNK]
            return u2 * w[0:1, :] + u1 * w[1:2, :] + u * w[2:3, :]

        def up_proj(n):
            c0 = n * FFN_CHUNK
            ug = jnp.dot(xn2, wup_ref[:, c0:c0 + FFN_CHUNK], preferred_element_type=F32)
            uv = jnp.dot(xn2, wup_ref[:, D_FF + c0:D_FF + c0 + FFN_CHUNK],
                         preferred_element_type=F32)
            return ug, uv

        nchunk = D_FF // FFN_CHUNK
        out_ref[0] = h1_ref[cur]
        ahead = [up_proj(n) for n in range(min(FFN_LOOKAHEAD, nchunk))]
        acts = []
        for n in range(nchunk):
            c0 = n * FFN_CHUNK
            ug, uv = ahead.pop(0)
            if n + FFN_LOOKAHEAD < nchunk:
                ahead.append(up_proj(n + FFN_LOOKAHEAD))
            if n == MERGE_AT_CHUNK:
                merged_ref[...] = merge_next()
            if n == NORM_AT_CHUNK:
                norm_next(merged_ref[...])
            gt = causal_conv(ug, gbuf_ref.at[n], c0)
            val = causal_conv(uv, vbuf_ref.at[n], D_FF + c0)
            acts.append((gt * jax.nn.sigmoid(gt) * val).astype(BF16))
            if len(acts) == DOWN_GROUP or n == nchunk - 1:
                r0 = (n + 1 - len(acts)) * FFN_CHUNK
                act = acts[0] if len(acts) == 1 else jnp.concatenate(acts, axis=1)
                out_ref[0] += jnp.dot(act, wdn_ref[r0:r0 + act.shape[1], :],
                                      preferred_element_type=F32)
                acts = []

        if NORM_AT_CHUNK >= nchunk:
            norm_next(merged_ref[...])
        out_ref[0] = _rms(out_ref[0], gf_ref[...])


def _sparse_attention(iqT, iwT, qT, ik, k, vT, *, topk, cast_along=()):
    B, _, S = qT.shape
    nq = S // Q_TILE
    featT = lambda c: pl.BlockSpec((1, c, Q_TILE), lambda b, j: (b, 0, j))

    def rows_spec(w):
        rows = next(r for r in range(BF16_ROWS, w.shape[0] + 1, BF16_ROWS)
                    if w.shape[0] % r == 0 and w.shape[0] // r <= B * nq)
        last = w.shape[0] // rows - 1
        return pl.BlockSpec((rows, w.shape[1]), lambda b, j: (jnp.minimum(b * nq + j, last), 0))

    cast_specs = [rows_spec(w) for w in cast_along]
    outs = pl.pallas_call(
        functools.partial(_attn_kernel, seq_len=S, topk=topk, n_cast=len(cast_along)),
        grid=(B, nq),
        in_specs=[featT(IDX_HEADS * IDX_DIM), featT(IDX_HEADS), featT(ATTN_WIDTH),
                  pl.BlockSpec((1, S, IDX_DIM), lambda b, j: (b, 0, 0)),
                  pl.BlockSpec((1, N_HEADS, S, HEAD_DIM), lambda b, j: (b, 0, 0, 0)),
                  pl.BlockSpec((1, ATTN_WIDTH, S), lambda b, j: (b, 0, 0))] + cast_specs,
        out_specs=[pl.BlockSpec((1, Q_TILE, ATTN_WIDTH), lambda b, j: (b, j, 0))] + cast_specs,
        out_shape=[jax.ShapeDtypeStruct((B, S, ATTN_WIDTH), BF16)]
        + [jax.ShapeDtypeStruct(w.shape, BF16) for w in cast_along],
        scratch_shapes=[pltpu.VMEM((S, Q_TILE), F32),
                        pltpu.VMEM((S // K_TILE, KEY_BITS, SUBLANES, Q_TILE), I32),
                        pltpu.VMEM((2, K_TILE, Q_TILE), F32),
                        pltpu.VMEM((N_HEADS, S, Q_TILE), F32),
                        pltpu.VMEM((ATTN_WIDTH, Q_TILE), F32),
                        pltpu.VMEM((SUBLANES, Q_TILE), I32),
                        pltpu.VMEM((S // K_TILE, SUBLANES, Q_TILE), I32)],
        compiler_params=pltpu.CompilerParams(
            dimension_semantics=("arbitrary", "arbitrary"), vmem_limit_bytes=VMEM_LIMIT),
        name="attn",
    )(iqT, iwT, qT, ik, k, vT, *cast_along)
    return outs[0], tuple(outs[1:])


def _const_spec(shape):
    nd = len(shape)
    return pl.BlockSpec(shape, lambda *_: (0,) * nd, pipeline_mode=pl.Buffered(1))


def kernel(x, norm_mix_g, w_in, b_gate, w_attn_out, conv_w, w_conv_out, w_o,
           norm_ffn_g, w_up, ffn_conv_w, w_down, norm_final_g):
    B, S, D = x.shape
    assert D == D_MODEL and S % PROJ_TILE == 0 and S % FFN_TILE == 0
    assert S % Q_TILE == 0 and Q_TILE == K_TILE
    assert norm_mix_g.shape[0] == 1, "single-layer block"
    topk = min(INDEX_TOPK_MAX, S // 4)
    tm = PROJ_TILE

    w = w_in[0]
    o_q, o_k, o_v, o_iq = 0, ATTN_WIDTH, 2 * ATTN_WIDTH, 3 * ATTN_WIDTH
    o_ik = o_iq + IDX_HEADS * IDX_DIM
    o_iw = o_ik + IDX_DIM
    o_ch = o_iw + IDX_HEADS
    o_g = o_ch + 3 * CONV_WIDTH
    wqT = w[:, o_q:o_k].T.astype(BF16)
    wvT = w[:, o_v:o_iq].T.astype(BF16)
    wiqT = w[:, o_iq:o_ik].T.astype(BF16)
    wiwT = jnp.pad(w[:, o_iw:o_ch].T, ((0, BF16_ROWS - IDX_HEADS), (0, 0))).astype(BF16)
    wk = w[:, o_k:o_v].astype(BF16)
    wik = jnp.pad(w[:, o_ik:o_iw], ((0, 0), (0, LANES - IDX_DIM))).astype(BF16)
    wconv = w[:, o_ch:o_g].astype(BF16)
    wg = w[:, o_g:].astype(BF16)
    g1 = norm_mix_g.reshape(1, D)
    bg = b_gate.reshape(1, 2 * D)
    cw = conv_w[0]
    wco = w_conv_out[0].astype(BF16)

    cparams = pltpu.CompilerParams(
        dimension_semantics=("arbitrary", "arbitrary"), vmem_limit_bytes=VMEM_LIMIT)

    tok = lambda c, t: pl.BlockSpec((1, t, c), lambda b, j: (b, j, 0))
    featT = lambda c, t: pl.BlockSpec((1, c, t), lambda b, j: (b, 0, j))

    k, ik, qT, vT, iqT, iwT, ga, mc = pl.pallas_call(
        _proj_kernel,
        grid=(B, S // tm),
        in_specs=[tok(D, tm), _const_spec((1, D)),
                  _const_spec(wk.shape), _const_spec(wik.shape),
                  _const_spec(wconv.shape), _const_spec(wg.shape),
                  _const_spec(wqT.shape), _const_spec(wvT.shape),
                  _const_spec(wiqT.shape), _const_spec(wiwT.shape),
                  _const_spec(bg.shape), _const_spec(cw.shape), _const_spec(wco.shape)],
        out_specs=[pl.BlockSpec((1, N_HEADS, tm, HEAD_DIM), lambda b, j: (b, 0, j, 0)),
                   tok(IDX_DIM, tm), featT(ATTN_WIDTH, tm), featT(ATTN_WIDTH, tm),
                   featT(IDX_HEADS * IDX_DIM, tm), featT(IDX_HEADS, tm),
                   tok(D, tm), tok(D, tm)],
        out_shape=[jax.ShapeDtypeStruct((B, N_HEADS, S, HEAD_DIM), BF16),
                   jax.ShapeDtypeStruct((B, S, IDX_DIM), BF16),
                   jax.ShapeDtypeStruct((B, ATTN_WIDTH, S), BF16),
                   jax.ShapeDtypeStruct((B, ATTN_WIDTH, S), BF16),
                   jax.ShapeDtypeStruct((B, IDX_HEADS * IDX_DIM, S), BF16),
                   jax.ShapeDtypeStruct((B, IDX_HEADS, S), F32),
                   jax.ShapeDtypeStruct((B, S, D), F32),
                   jax.ShapeDtypeStruct((B, S, D), F32)],
        scratch_shapes=[pltpu.VMEM((tm + SUBLANES, CONV_WIDTH), F32)],
        compiler_params=cparams,
        name="proj",
    )(x, g1, wk, wik, wconv, wg, wqT, wvT, wiqT, wiwT, bg, cw, wco)

    attn, (wao, wo, wup, wdn) = _sparse_attention(
        iqT, iwT, qT, ik, k, vT, topk=topk,
        cast_along=(w_attn_out[0], w_o[0], w_up[0], w_down[0]))
    fcw = ffn_conv_w[0]
    g2 = norm_ffn_g.reshape(1, D)
    gf = norm_final_g.reshape(1, D)
    nchunk = D_FF // FFN_CHUNK
    assert nchunk * FFN_CHUNK == D_FF
    tf = FFN_TILE

    nt = S // tf
    n_tiles = B * nt

    def tile_in(c):
        def index(t):
            tt = jnp.minimum(t, n_tiles - 1)
            return (tt // nt, tt % nt, 0)
        return pl.BlockSpec((1, tf, c), index)

    def tile_out(t):
        tt = jnp.maximum(t - 1, 0)
        return (tt // nt, tt % nt, 0)

    out = pl.pallas_call(
        functools.partial(_ffn_kernel, tiles_per_seq=nt),
        grid=(n_tiles + 1,),
        in_specs=[tile_in(D), tile_in(ATTN_WIDTH), tile_in(D), tile_in(D),
                  _const_spec(wao.shape), _const_spec(wo.shape), _const_spec(g2.shape),
                  _const_spec(wup.shape), _const_spec(fcw.shape), _const_spec(wdn.shape),
                  _const_spec(gf.shape)],
        out_specs=pl.BlockSpec((1, tf, D), tile_out),
        out_shape=jax.ShapeDtypeStruct((B, S, D), F32),
        scratch_shapes=[pltpu.VMEM((nchunk, tf + SUBLANES, FFN_CHUNK), F32),
                        pltpu.VMEM((nchunk, tf + SUBLANES, FFN_CHUNK), F32),
                        pltpu.VMEM((2, tf, D), F32),
                        pltpu.VMEM((2, tf, D), BF16),
                        pltpu.VMEM((tf, D), BF16)],
        compiler_params=pltpu.CompilerParams(
            dimension_semantics=("arbitrary",), vmem_limit_bytes=VMEM_LIMIT),
        name="ffn",
    )(x, attn, ga, mc, wao, wo, g2, wup, fcw, wdn, gf)
    return out
```

```python
import functools

import jax
import jax.numpy as jnp
from jax import lax
from jax.experimental import pallas as pl
from jax.experimental.pallas import tpu as pltpu

D_MODEL = 1024
N_HEADS = 8
HEAD_DIM = 64
ATTN_WIDTH = N_HEADS * HEAD_DIM
IDX_HEADS = 8
IDX_DIM = 64
INDEX_TOPK_MAX = 256
CONV_WIDTH = 512
D_FF = 2816
EPS = 1e-6
LOG2E = 1.4426950408889634

F32 = jnp.float32
BF16 = jnp.bfloat16
I32 = jnp.int32
KEY_BITS = 32
INT_MIN = -(2 ** 31)

SUBLANES = 8
BF16_ROWS = 16
LANES = 128
MXU_DIM = 256
VMEM_LIMIT = 56 * 1024 * 1024

PROJ_TILE = 512
FFN_TILE = 256
Q_TILE = MXU_DIM
K_TILE = 256
SCORE_ROWS = 64
EARLY_QK_HEADS = 4
FFN_CHUNK = 256
FFN_LOOKAHEAD = 2
DOWN_GROUP = 4
MERGE_AT_CHUNK = 8
NORM_AT_CHUNK = 10

_NT = (((1,), (1,)), ((), ()))


def _rms(x, g):
    return x * lax.rsqrt(jnp.mean(x * x, axis=-1, keepdims=True) + EPS) * g


def _shifted_rows(buf_ref, u, first_tile):
    tm = u.shape[0]

    @pl.when(first_tile)
    def _():
        buf_ref[0:SUBLANES, :] = jnp.zeros((SUBLANES, u.shape[1]), F32)

    buf_ref[SUBLANES:SUBLANES + tm, :] = u
    u1 = buf_ref[SUBLANES - 1:SUBLANES - 1 + tm, :]
    u2 = buf_ref[SUBLANES - 2:SUBLANES - 2 + tm, :]
    buf_ref[0:SUBLANES, :] = u[tm - SUBLANES:, :]
    return u1, u2


def _proj_kernel(x_ref, g_ref, wk_ref, wik_ref, wconv_ref, wg_ref,
                 wqT_ref, wvT_ref, wiqT_ref, wiwT_ref,
                 bg_ref, cw_ref, wco_ref,
                 k_ref, ik_ref, qT_ref, vT_ref, iqT_ref, iwT_ref, ga_ref, mc_ref,
                 ubuf_ref):
    j = pl.program_id(1)
    xn = _rms(x_ref[0], g_ref[...]).astype(BF16)

    pc = jnp.dot(xn, wconv_ref[...], preferred_element_type=F32)
    ch = pc[:, 0:CONV_WIDTH]
    cb = pc[:, CONV_WIDTH:2 * CONV_WIDTH]
    cc = pc[:, 2 * CONV_WIDTH:3 * CONV_WIDTH]
    u = cc * ch
    u1, u2 = _shifted_rows(ubuf_ref, u, j == 0)
    cw = cw_ref[...]
    conv = u2 * cw[0:1, :] + u1 * cw[1:2, :] + u * cw[2:3, :]
    cbconv = (cb * conv).astype(BF16)

    g = jax.nn.sigmoid(jnp.dot(xn, wg_ref[...], preferred_element_type=F32)
                       + bg_ref[...])
    ga_ref[0] = g[:, 0:D_MODEL]

    qT = lax.dot_general(wqT_ref[...], xn, _NT, preferred_element_type=F32)
    qT_ref[0] = (qT * (HEAD_DIM ** -0.5 * LOG2E)).astype(BF16)
    vT = lax.dot_general(wvT_ref[...], xn, _NT, preferred_element_type=F32)
    vT_ref[0] = vT.astype(BF16)
    iqT = lax.dot_general(wiqT_ref[...], xn, _NT, preferred_element_type=F32)
    iqT_ref[0] = iqT.astype(BF16)
    iwT = lax.dot_general(wiwT_ref[...], xn, _NT, preferred_element_type=F32)
    iwT_ref[0] = iwT[0:IDX_HEADS, :]

    kk = jnp.dot(xn, wk_ref[...], preferred_element_type=F32)
    for h in range(N_HEADS):
        k_ref[0, h] = kk[:, h * HEAD_DIM:(h + 1) * HEAD_DIM].astype(BF16)
    ik = jnp.dot(xn, wik_ref[...], preferred_element_type=F32)
    ik_ref[0] = ik[:, 0:IDX_DIM].astype(BF16)

    bc = jnp.dot(cbconv, wco_ref[...], preferred_element_type=F32)
    mc_ref[0] = g[:, D_MODEL:2 * D_MODEL] * bc


def _bit_transpose32(a):
    rows = [a[j] for j in range(KEY_BITS)]
    j, m = KEY_BITS // 2, 0x0000FFFF
    while j:
        for k in range(KEY_BITS):
            if k & j:
                continue
            t = (rows[k] ^ lax.shift_right_logical(rows[k + j], jnp.int32(j))) & m
            rows[k] = rows[k] ^ t
            rows[k + j] = rows[k + j] ^ lax.shift_left(t, jnp.int32(j))
        j >>= 1
        m = (m ^ (m << j)) & 0xFFFFFFFF
    return jnp.stack(rows)


def _attn_kernel(iqT_ref, iwT_ref, qT_ref, ik_ref, k_ref, vT_ref, *rest, seq_len, topk, n_cast):
    cast_in, (o_ref, *rest) = rest[:n_cast], rest[n_cast:]
    cast_out, scratch = rest[:n_cast], rest[n_cast:]
    sc_ref, planes_ref, bias_ref, l_ref, oT_ref, tlim_ref, ties_ref = scratch
    for src, dst in zip(cast_in, cast_out):
        dst[...] = src[...].astype(BF16)
    assert K_TILE == KEY_BITS * SUBLANES
    max_chunks = seq_len // K_TILE
    i = pl.program_id(1)
    nch = i + 1
    row_k = lax.broadcasted_iota(I32, (K_TILE, Q_TILE), 0)
    lane_q = lax.broadcasted_iota(I32, (K_TILE, Q_TILE), 1)
    rel = row_k - lane_q

    def chunk_start(c):
        return pl.multiple_of(c * K_TILE, K_TILE)

    def chunk_loop(body, init):
        def pair(p, carry):
            return body(2 * p + 1, 1, body(2 * p, 0, carry))
        carry = lax.fori_loop(0, lax.shift_right_logical(nch, 1), pair, init)
        return lax.cond((nch & 1) == 1, lambda cr: body(nch - 1, 0, cr), lambda cr: cr, carry)

    def qk_logits(h, r0):
        return jnp.dot(k_ref[0, h, pl.ds(r0, K_TILE), :],
                       qT_ref[0, h * HEAD_DIM:(h + 1) * HEAD_DIM, :],
                       preferred_element_type=F32)

    def fold_rows(m):
        return jnp.sum(m.reshape(K_TILE // SUBLANES, SUBLANES, Q_TILE), axis=0)

    iw = iwT_ref[0] * ((IDX_HEADS ** -0.5) * (IDX_DIM ** -0.5))

    def score_chunk(c, slot, carry):
        for sb in range(K_TILE // SCORE_ROWS):
            r0 = pl.multiple_of(c * K_TILE + sb * SCORE_ROWS, SCORE_ROWS)
            ikc = ik_ref[0, pl.ds(r0, SCORE_ROWS), :]
            acc = jnp.zeros((SCORE_ROWS, Q_TILE), F32)
            for h in range(IDX_HEADS):
                iq_h = iqT_ref[0, h * IDX_DIM:(h + 1) * IDX_DIM, :]
                d = jnp.dot(ikc, iq_h, preferred_element_type=F32)
                acc = acc + jnp.maximum(d, 0.0) * iw[h:h + 1, :]
            acc = jnp.where(acc == 0.0, 0.0, acc)
            valid = (rel[0:SCORE_ROWS, :] + sb * SCORE_ROWS) <= (i - c) * K_TILE
            sc_ref[pl.ds(r0, SCORE_ROWS), :] = jnp.where(valid, acc, -jnp.inf)
        for h in range(EARLY_QK_HEADS):
            l_ref[h, pl.ds(chunk_start(c), K_TILE), :] = qk_logits(h, chunk_start(c))
        for lt in range(Q_TILE // LANES):
            ls = slice(lt * LANES, (lt + 1) * LANES)
            bits = pltpu.bitcast(sc_ref[pl.ds(chunk_start(c), K_TILE), ls], I32)
            u = bits ^ ((bits >> 31) | INT_MIN)
            planes_ref[c, :, :, ls] = _bit_transpose32(u.reshape(KEY_BITS, SUBLANES, LANES))
        return carry

    @pl.when(jnp.logical_and(pl.program_id(0) == 0, i == 0))
    def _():
        planes_ref[...] = jnp.zeros(planes_ref.shape, I32)

    chunk_loop(score_chunk, 0)

    def count(words):
        pops = [lax.population_count(w) for w in words]
        while len(pops) > 1:
            pops = [a + b for a, b in zip(pops[0::2], pops[1::2])]
        return jnp.sum(pops[0], axis=0, keepdims=True)

    def radix_step(r, carry):
        match, n_gt, thr_u = carry
        ones = [match[c] & planes_ref[c, r] for c in range(max_chunks)]
        n_one = count(ones)
        take = (n_gt + n_one) >= topk
        thr_u = jnp.where(take, thr_u | lax.shift_left(jnp.int32(1), KEY_BITS - 1 - r), thr_u)
        n_gt = jnp.where(take, n_gt, n_gt + n_one)
        match = tuple(jnp.where(take, ones[c], match[c] ^ ones[c]) for c in range(max_chunks))
        return match, n_gt, thr_u

    def radix_threshold():
        match0 = tuple(jnp.broadcast_to(jnp.where(c < nch, -1, 0), (SUBLANES, Q_TILE)).astype(I32)
                       for c in range(max_chunks))
        _, _, thr_u = lax.fori_loop(
            0, KEY_BITS, radix_step,
            (match0, jnp.zeros((1, Q_TILE), I32), jnp.zeros((1, Q_TILE), I32)))
        key = thr_u ^ INT_MIN
        thr = pltpu.bitcast(key ^ ((key >> 31) & 0x7FFFFFFF), F32)
        return jnp.where(thr > -jnp.inf, thr, -jnp.inf)

    def rank_counts(t):
        def body(c, slot, acc):
            sc = sc_ref[pl.ds(chunk_start(c), K_TILE), :]
            code = jnp.where(sc > t, 1 << 16, jnp.where(sc == t, 1, 0)).astype(I32)
            return acc + fold_rows(code)
        both = jnp.sum(chunk_loop(body, jnp.zeros((SUBLANES, Q_TILE), I32)),
                       axis=0, keepdims=True)
        gt = both >> 16
        return gt, gt + (both & 0xFFFF)

    def off_by(state):
        _, gt, ge, steps = state
        bad = jnp.logical_or(gt >= topk, ge < topk)
        return jnp.logical_and(jnp.max(jnp.where(bad, 1, 0)) > 0, steps < topk)

    def walk(state):
        t, gt, ge, steps = state

        lower = lambda a, b: jnp.where(a < b, a, b)
        upper = lambda a, b: jnp.where(a > b, a, b)

        def fold(op, x):
            while x.shape[0] > 1:
                half = x.shape[0] // 2
                x = op(x[:half], x[half:])
            return x

        def body(c, slot, carry):
            sc = sc_ref[pl.ds(chunk_start(c), K_TILE), :]
            return (lower(carry[0], fold(lower, jnp.where(sc > t, sc, jnp.inf))),
                    upper(carry[1], fold(upper, jnp.where(sc < t, sc, -jnp.inf))))
        up, dn = chunk_loop(body, (jnp.full((1, Q_TILE), jnp.inf, F32),
                                   jnp.full((1, Q_TILE), -jnp.inf, F32)))
        t = jnp.where(gt >= topk, up, jnp.where(ge < topk, dn, t))
        return (t,) + rank_counts(t) + (steps + 1,)

    def attend(thr, tlim, staged_heads, tile_flag):
        def logits_chunk(c, slot, m8):
            r0 = chunk_start(c)
            sc = sc_ref[pl.ds(r0, K_TILE), :]
            tie_ok = jnp.logical_and(sc == thr, (row_k + r0) <= tlim)
            sel = jnp.logical_or(sc > thr, tie_ok)
            bias_ref[slot] = jnp.where(sel, 0.0, -jnp.inf).astype(F32)
            out = []
            for h in range(N_HEADS):
                raw = l_ref[h, pl.ds(r0, K_TILE), :] if h < staged_heads else qk_logits(h, r0)
                l = raw + bias_ref[slot]
                l_ref[h, pl.ds(r0, K_TILE), :] = l
                out.append(jnp.maximum(
                    m8[h], jnp.max(l.reshape(K_TILE // SUBLANES, SUBLANES, Q_TILE), axis=0)))
            return tuple(out)

        m8 = chunk_loop(
            logits_chunk,
            tuple(jnp.full((SUBLANES, Q_TILE), -jnp.inf, F32) for _ in range(N_HEADS)))
        m_all = [jnp.max(m, axis=0, keepdims=True) for m in m8]

        oT_ref[...] = jnp.zeros(oT_ref.shape, F32)

        def pv_chunk(c, slot, s8):
            r0 = chunk_start(c)
            out = []
            for h in range(N_HEADS):
                hs = slice(h * HEAD_DIM, (h + 1) * HEAD_DIM)
                p = jnp.exp2(l_ref[h, pl.ds(r0, K_TILE), :] - m_all[h])
                out.append(s8[h] + fold_rows(p))
                oT_ref[hs, :] += jnp.dot(vT_ref[0, hs, pl.ds(r0, K_TILE)], p.astype(BF16),
                                         preferred_element_type=F32)
            return tuple(out)

        s8 = chunk_loop(
            pv_chunk, tuple(jnp.zeros((SUBLANES, Q_TILE), F32) for _ in range(N_HEADS)))
        flag = tile_flag()
        for h in range(N_HEADS):
            hs = slice(h * HEAD_DIM, (h + 1) * HEAD_DIM)
            oT_ref[hs, :] = oT_ref[hs, :] / jnp.sum(s8[h], axis=0, keepdims=True)
        o_ref[0] = oT_ref[...].T.astype(BF16)
        return flag

    thr0, gt0, ge0 = lax.cond(
        (i + 1) * Q_TILE > topk, lambda: (lambda t: (t,) + rank_counts(t))(radix_threshold()),
        lambda: (jnp.full((1, Q_TILE), -jnp.inf, F32), jnp.zeros((1, Q_TILE), I32),
                 jnp.full((1, Q_TILE), topk, I32)))
    floored = lambda t: jnp.where(t > -jnp.inf, t, jnp.finfo(F32).min)

    def limit_ties_and_attend(thr, n_gt, n_ge, staged_heads, tile_flag):
        n_eq = n_ge - n_gt
        need = topk - n_gt
        has_k = thr > -jnp.inf
        thr = floored(thr)

        tlim_ref[...] = jnp.full((SUBLANES, Q_TILE), seq_len, I32)

        @pl.when(jnp.max(jnp.where(has_k, n_eq - need, 0)) > 0)
        def _():
            sub = lax.broadcasted_iota(I32, (SUBLANES, Q_TILE), 0)

            ties_ref[...] = jnp.zeros(ties_ref.shape, I32)

            def pack_chunk(c, slot, carry):
                eq = (sc_ref[pl.ds(chunk_start(c), K_TILE), :] == thr).reshape(
                    KEY_BITS, SUBLANES, Q_TILE)
                word = jnp.zeros((SUBLANES, Q_TILE), I32)
                for g in range(KEY_BITS):
                    word = word | jnp.where(
                        eq[g], INT_MIN if g == 0 else 1 << (KEY_BITS - 1 - g), 0)
                ties_ref[c] = word
                return carry

            chunk_loop(pack_chunk, 0)
            ties = [ties_ref[c] for c in range(max_chunks)]

            def ties_before(lim):
                tot = jnp.zeros((SUBLANES, Q_TILE), I32)
                for c in range(max_chunks):
                    g_last = (lim - (c * K_TILE + 1) - sub) >> 3
                    below = jnp.where(
                        g_last < 0, 0,
                        lax.shift_left(jnp.int32(-1),
                                       KEY_BITS - 1 - jnp.minimum(g_last, KEY_BITS - 1)))
                    tot = tot + lax.population_count(ties[c] & below)
                return jnp.sum(tot, axis=0, keepdims=True)

            def tie_step(t, lim):
                bit = lax.shift_left(jnp.int32(1), (seq_len.bit_length() - 2) - t)
                cand = lim | bit
                return jnp.where(ties_before(cand) < need, cand, lim)
            lim = lax.fori_loop(0, seq_len.bit_length() - 1, tie_step,
                                jnp.zeros((1, Q_TILE), I32))
            tlim_ref[...] = jnp.broadcast_to(jnp.where(n_eq > need, lim, seq_len),
                                             (SUBLANES, Q_TILE))

        return attend(thr, tlim_ref[0:1, :], staged_heads, tile_flag)

    off = jnp.logical_or(gt0 >= topk, ge0 < topk)
    redo = limit_ties_and_attend(thr0, gt0, ge0, EARLY_QK_HEADS,
                                 lambda: jnp.max(jnp.where(off, 1, 0)))

    @pl.when(redo > 0)
    def _():
        limit_ties_and_attend(
            *lax.while_loop(off_by, walk, (thr0, gt0, ge0, jnp.int32(0)))[:3], 0, lambda: None)


def _ffn_kernel(x_ref, attn_ref, ga_ref, mc_ref, wao_ref, wo_ref, g2_ref,
                wup_ref, fcw_ref, wdn_ref, gf_ref, out_ref,
                gbuf_ref, vbuf_ref, h1_ref, xn2_ref, merged_ref, *, tiles_per_seq):
    t = pl.program_id(0)
    tm = x_ref.shape[1]
    cur = lax.rem(t + 1, 2)
    nxt = lax.rem(t, 2)

    @pl.when(lax.rem(t - 1, tiles_per_seq) == 0)
    def _():
        zeros = jnp.zeros((SUBLANES, FFN_CHUNK), F32)
        for n in range(D_FF // FFN_CHUNK):
            gbuf_ref[n, 0:SUBLANES, :] = zeros
            vbuf_ref[n, 0:SUBLANES, :] = zeros

    def merge_next():
        ba = jnp.dot(attn_ref[0], wao_ref[...], preferred_element_type=F32)
        return (ga_ref[0] * ba + mc_ref[0]).astype(BF16)

    def norm_next(merged):
        h1n = x_ref[0] + jnp.dot(merged, wo_ref[...], preferred_element_type=F32)
        h1_ref[nxt] = h1n
        xn2_ref[nxt] = _rms(h1n, g2_ref[...]).astype(BF16)

    @pl.when(t == 0)
    def _():
        norm_next(merge_next())

    @pl.when(t > 0)
    def _():
        xn2 = xn2_ref[cur]

        def causal_conv(u, buf_ref, col0):
            buf_ref[SUBLANES:SUBLANES + tm, :] = u
            u1 = buf_ref[SUBLANES - 1:SUBLANES - 1 + tm, :]
            u2 = buf_ref[SUBLANES - 2:SUBLANES - 2 + tm, :]
            buf_ref[0:SUBLANES, :] = u[tm - SUBLANES:, :]
            w = fcw_ref[:, col0:col0 + FFN_CHUNK]
            return u2 * w[0:1, :] + u1 * w[1:2, :] + u * w[2:3, :]

        def up_proj(n):
            c0 = n * FFN_CHUNK
            ug = jnp.dot(xn2, wup_ref[:, c0:c0 + FFN_CHUNK], preferred_element_type=F32)
            uv = jnp.dot(xn2, wup_ref[:, D_FF + c0:D_FF + c0 + FFN_CHUNK],
                         preferred_element_type=F32)
            return ug, uv

        nchunk = D_FF // FFN_CHUNK
        out_ref[0] = h1_ref[cur]
        ahead = [up_proj(n) for n in range(min(FFN_LOOKAHEAD, nchunk))]
        acts = []
        for n in range(nchunk):
            c0 = n * FFN_CHUNK
            ug, uv = ahead.pop(0)
            if n + FFN_LOOKAHEAD < nchunk:
                ahead.append(up_proj(n + FFN_LOOKAHEAD))
            if n == MERGE_AT_CHUNK:
                merged_ref[...] = merge_next()
            if n == NORM_AT_CHUNK:
                norm_next(merged_ref[...])
            gt = causal_conv(ug, gbuf_ref.at[n], c0)
            val = causal_conv(uv, vbuf_ref.at[n], D_FF + c0)
            acts.append((gt * jax.nn.sigmoid(gt) * val).astype(BF16))
            if len(acts) == DOWN_GROUP or n == nchunk - 1:
                r0 = (n + 1 - len(acts)) * FFN_CHUNK
                act = acts[0] if len(acts) == 1 else jnp.concatenate(acts, axis=1)
                out_ref[0] += jnp.dot(act, wdn_ref[r0:r0 + act.shape[1], :],
                                      preferred_element_type=F32)
                acts = []

        if NORM_AT_CHUNK >= nchunk:
            norm_next(merged_ref[...])
        out_ref[0] = _rms(out_ref[0], gf_ref[...])


def _sparse_attention(iqT, iwT, qT, ik, k, vT, *, topk, cast_along=()):
    B, _, S = qT.shape
    nq = S // Q_TILE
    featT = lambda c: pl.BlockSpec((1, c, Q_TILE), lambda b, j: (b, 0, j))

    def rows_spec(w):
        rows = next(r for r in range(BF16_ROWS, w.shape[0] + 1, BF16_ROWS)
                    if w.shape[0] % r == 0 and w.shape[0] // r <= B * nq)
        last = w.shape[0] // rows - 1
        return pl.BlockSpec((rows, w.shape[1]), lambda b, j: (jnp.minimum(b * nq + j, last), 0))

    cast_specs = [rows_spec(w) for w in cast_along]
    outs = pl.pallas_call(
        functools.partial(_attn_kernel, seq_len=S, topk=topk, n_cast=len(cast_along)),
        grid=(B, nq),
        in_specs=[featT(IDX_HEADS * IDX_DIM), featT(IDX_HEADS), featT(ATTN_WIDTH),
                  pl.BlockSpec((1, S, IDX_DIM), lambda b, j: (b, 0, 0)),
                  pl.BlockSpec((1, N_HEADS, S, HEAD_DIM), lambda b, j: (b, 0, 0, 0)),
                  pl.BlockSpec((1, ATTN_WIDTH, S), lambda b, j: (b, 0, 0))] + cast_specs,
        out_specs=[pl.BlockSpec((1, Q_TILE, ATTN_WIDTH), lambda b, j: (b, j, 0))] + cast_specs,
        out_shape=[jax.ShapeDtypeStruct((B, S, ATTN_WIDTH), BF16)]
        + [jax.ShapeDtypeStruct(w.shape, BF16) for w in cast_along],
        scratch_shapes=[pltpu.VMEM((S, Q_TILE), F32),
                        pltpu.VMEM((S // K_TILE, KEY_BITS, SUBLANES, Q_TILE), I32),
                        pltpu.VMEM((2, K_TILE, Q_TILE), F32),
                        pltpu.VMEM((N_HEADS, S, Q_TILE), F32),
                        pltpu.VMEM((ATTN_WIDTH, Q_TILE), F32),
                        pltpu.VMEM((SUBLANES, Q_TILE), I32),
                        pltpu.VMEM((S // K_TILE, SUBLANES, Q_TILE), I32)],
        compiler_params=pltpu.CompilerParams(
            dimension_semantics=("arbitrary", "arbitrary"), vmem_limit_bytes=VMEM_LIMIT),
        name="attn",
    )(iqT, iwT, qT, ik, k, vT, *cast_along)
    return outs[0], tuple(outs[1:])


def _const_spec(shape):
    nd = len(shape)
    return pl.BlockSpec(shape, lambda *_: (0,) * nd, pipeline_mode=pl.Buffered(1))


def kernel(x, norm_mix_g, w_in, b_gate, w_attn_out, conv_w, w_conv_out, w_o,
           norm_ffn_g, w_up, ffn_conv_w, w_down, norm_final_g):
    B, S, D = x.shape
    assert D == D_MODEL and S % PROJ_TILE == 0 and S % FFN_TILE == 0
    assert S % Q_TILE == 0 and Q_TILE == K_TILE
    assert norm_mix_g.shape[0] == 1, "single-layer block"
    topk = min(INDEX_TOPK_MAX, S // 4)
    tm = PROJ_TILE

    w = w_in[0]
    o_q, o_k, o_v, o_iq = 0, ATTN_WIDTH, 2 * ATTN_WIDTH, 3 * ATTN_WIDTH
    o_ik = o_iq + IDX_HEADS * IDX_DIM
    o_iw = o_ik + IDX_DIM
    o_ch = o_iw + IDX_HEADS
    o_g = o_ch + 3 * CONV_WIDTH
    wqT = w[:, o_q:o_k].T.astype(BF16)
    wvT = w[:, o_v:o_iq].T.astype(BF16)
    wiqT = w[:, o_iq:o_ik].T.astype(BF16)
    wiwT = jnp.pad(w[:, o_iw:o_ch].T, ((0, BF16_ROWS - IDX_HEADS), (0, 0))).astype(BF16)
    wk = w[:, o_k:o_v].astype(BF16)
    wik = jnp.pad(w[:, o_ik:o_iw], ((0, 0), (0, LANES - IDX_DIM))).astype(BF16)
    wconv = w[:, o_ch:o_g].astype(BF16)
    wg = w[:, o_g:].astype(BF16)
    g1 = norm_mix_g.reshape(1, D)
    bg = b_gate.reshape(1, 2 * D)
    cw = conv_w[0]
    wco = w_conv_out[0].astype(BF16)

    cparams = pltpu.CompilerParams(
        dimension_semantics=("arbitrary", "arbitrary"), vmem_limit_bytes=VMEM_LIMIT)

    tok = lambda c, t: pl.BlockSpec((1, t, c), lambda b, j: (b, j, 0))
    featT = lambda c, t: pl.BlockSpec((1, c, t), lambda b, j: (b, 0, j))

    k, ik, qT, vT, iqT, iwT, ga, mc = pl.pallas_call(
        _proj_kernel,
        grid=(B, S // tm),
        in_specs=[tok(D, tm), _const_spec((1, D)),
                  _const_spec(wk.shape), _const_spec(wik.shape),
                  _const_spec(wconv.shape), _const_spec(wg.shape),
                  _const_spec(wqT.shape), _const_spec(wvT.shape),
                  _const_spec(wiqT.shape), _const_spec(wiwT.shape),
                  _const_spec(bg.shape), _const_spec(cw.shape), _const_spec(wco.shape)],
        out_specs=[pl.BlockSpec((1, N_HEADS, tm, HEAD_DIM), lambda b, j: (b, 0, j, 0)),
                   tok(IDX_DIM, tm), featT(ATTN_WIDTH, tm), featT(ATTN_WIDTH, tm),
                   featT(IDX_HEADS * IDX_DIM, tm), featT(IDX_HEADS, tm),
                   tok(D, tm), tok(D, tm)],
        out_shape=[jax.ShapeDtypeStruct((B, N_HEADS, S, HEAD_DIM), BF16),
                   jax.ShapeDtypeStruct((B, S, IDX_DIM), BF16),
                   jax.ShapeDtypeStruct((B, ATTN_WIDTH, S), BF16),
                   jax.ShapeDtypeStruct((B, ATTN_WIDTH, S), BF16),
                   jax.ShapeDtypeStruct((B, IDX_HEADS * IDX_DIM, S), BF16),
                   jax.ShapeDtypeStruct((B, IDX_HEADS, S), F32),
                   jax.ShapeDtypeStruct((B, S, D), F32),
                   jax.ShapeDtypeStruct((B, S, D), F32)],
        scratch_shapes=[pltpu.VMEM((tm + SUBLANES, CONV_WIDTH), F32)],
        compiler_params=cparams,
        name="proj",
    )(x, g1, wk, wik, wconv, wg, wqT, wvT, wiqT, wiwT, bg, cw, wco)

    attn, (wao, wo, wup, wdn) = _sparse_attention(
        iqT, iwT, qT, ik, k, vT, topk=topk,
        cast_along=(w_attn_out[0], w_o[0], w_up[0], w_down[0]))
    fcw = ffn_conv_w[0]
    g2 = norm_ffn_g.reshape(1, D)
    gf = norm_final_g.reshape(1, D)
    nchunk = D_FF // FFN_CHUNK
    assert nchunk * FFN_CHUNK == D_FF
    tf = FFN_TILE

    nt = S // tf
    n_tiles = B * nt

    def tile_in(c):
        def index(t):
            tt = jnp.minimum(t, n_tiles - 1)
            return (tt // nt, tt % nt, 0)
        return pl.BlockSpec((1, tf, c), index)

    def tile_out(t):
        tt = jnp.maximum(t - 1, 0)
        return (tt // nt, tt % nt, 0)

    out = pl.pallas_call(
        functools.partial(_ffn_kernel, tiles_per_seq=nt),
        grid=(n_tiles + 1,),
        in_specs=[tile_in(D), tile_in(ATTN_WIDTH), tile_in(D), tile_in(D),
                  _const_spec(wao.shape), _const_spec(wo.shape), _const_spec(g2.shape),
                  _const_spec(wup.shape), _const_spec(fcw.shape), _const_spec(wdn.shape),
                  _const_spec(gf.shape)],
        out_specs=pl.BlockSpec((1, tf, D), tile_out),
        out_shape=jax.ShapeDtypeStruct((B, S, D), F32),
        scratch_shapes=[pltpu.VMEM((nchunk, tf + SUBLANES, FFN_CHUNK), F32),
                        pltpu.VMEM((nchunk, tf + SUBLANES, FFN_CHUNK), F32),
                        pltpu.VMEM((2, tf, D), F32),
                        pltpu.VMEM((2, tf, D), BF16),
                        pltpu.VMEM((tf, D), BF16)],
        compiler_params=pltpu.CompilerParams(
            dimension_semantics=("arbitrary",), vmem_limit_bytes=VMEM_LIMIT),
        name="ffn",
    )(x, attn, ga, mc, wao, wo, g2, wup, fcw, wdn, gf)
    return out
```

```python
import functools

import jax
import jax.numpy as jnp
from jax import lax
from jax.experimental import pallas as pl
from jax.experimental.pallas import tpu as pltpu

D_MODEL = 1024
N_HEADS = 8
HEAD_DIM = 64
ATTN_WIDTH = N_HEADS * HEAD_DIM
IDX_HEADS = 8
IDX_DIM = 64
INDEX_TOPK_MAX = 256
CONV_WIDTH = 512
D_FF = 2816
EPS = 1e-6
LOG2E = 1.4426950408889634

F32 = jnp.float32
BF16 = jnp.bfloat16
I32 = jnp.int32
KEY_BITS = 32
INT_MIN = -(2 ** 31)

SUBLANES = 8
BF16_ROWS = 16
LANES = 128
MXU_DIM = 256
VMEM_LIMIT = 56 * 1024 * 1024

PROJ_TILE = 512
FFN_TILE = 256
Q_TILE = MXU_DIM
K_TILE = 256
SCORE_ROWS = 64
EARLY_QK_HEADS = 4
FFN_CHUNK = 256
FFN_LOOKAHEAD = 2
DOWN_GROUP = 4
MERGE_AT_CHUNK = 8
NORM_AT_CHUNK = 10

_NT = (((1,), (1,)), ((), ()))


def _rms(x, g):
    return x * lax.rsqrt(jnp.mean(x * x, axis=-1, keepdims=True) + EPS) * g


def _shifted_rows(buf_ref, u, first_tile):
    tm = u.shape[0]

    @pl.when(first_tile)
    def _():
        buf_ref[0:SUBLANES, :] = jnp.zeros((SUBLANES, u.shape[1]), F32)

    buf_ref[SUBLANES:SUBLANES + tm, :] = u
    u1 = buf_ref[SUBLANES - 1:SUBLANES - 1 + tm, :]
    u2 = buf_ref[SUBLANES - 2:SUBLANES - 2 + tm, :]
    buf_ref[0:SUBLANES, :] = u[tm - SUBLANES:, :]
    return u1, u2


def _proj_kernel(x_ref, g_ref, wk_ref, wik_ref, wconv_ref, wg_ref,
                 wqT_ref, wvT_ref, wiqT_ref, wiwT_ref,
                 bg_ref, cw_ref, wco_ref,
                 k_ref, ik_ref, qT_ref, vT_ref, iqT_ref, iwT_ref, ga_ref, mc_ref,
                 ubuf_ref):
    j = pl.program_id(1)
    xn = _rms(x_ref[0], g_ref[...]).astype(BF16)

    pc = jnp.dot(xn, wconv_ref[...], preferred_element_type=F32)
    ch = pc[:, 0:CONV_WIDTH]
    cb = pc[:, CONV_WIDTH:2 * CONV_WIDTH]
    cc = pc[:, 2 * CONV_WIDTH:3 * CONV_WIDTH]
    u = cc * ch
    u1, u2 = _shifted_rows(ubuf_ref, u, j == 0)
    cw = cw_ref[...]
    conv = u2 * cw[0:1, :] + u1 * cw[1:2, :] + u * cw[2:3, :]
    cbconv = (cb * conv).astype(BF16)

    g = jax.nn.sigmoid(jnp.dot(xn, wg_ref[...], preferred_element_type=F32)
                       + bg_ref[...])
    ga_ref[0] = g[:, 0:D_MODEL]

    qT = lax.dot_general(wqT_ref[...], xn, _NT, preferred_element_type=F32)
    qT_ref[0] = (qT * (HEAD_DIM ** -0.5 * LOG2E)).astype(BF16)
    vT = lax.dot_general(wvT_ref[...], xn, _NT, preferred_element_type=F32)
    vT_ref[0] = vT.astype(BF16)
    iqT = lax.dot_general(wiqT_ref[...], xn, _NT, preferred_element_type=F32)
    iqT_ref[0] = iqT.astype(BF16)
    iwT = lax.dot_general(wiwT_ref[...], xn, _NT, preferred_element_type=F32)
    iwT_ref[0] = iwT[0:IDX_HEADS, :]

    kk = jnp.dot(xn, wk_ref[...], preferred_element_type=F32)
    for h in range(N_HEADS):
        k_ref[0, h] = kk[:, h * HEAD_DIM:(h + 1) * HEAD_DIM].astype(BF16)
    ik = jnp.dot(xn, wik_ref[...], preferred_element_type=F32)
    ik_ref[0] = ik[:, 0:IDX_DIM].astype(BF16)

    bc = jnp.dot(cbconv, wco_ref[...], preferred_element_type=F32)
    mc_ref[0] = g[:, D_MODEL:2 * D_MODEL] * bc


def _bit_transpose32(a):
    rows = [a[j] for j in range(KEY_BITS)]
    j, m = KEY_BITS // 2, 0x0000FFFF
    while j:
        for k in range(KEY_BITS):
            if k & j:
                continue
            t = (rows[k] ^ lax.shift_right_logical(rows[k + j], jnp.int32(j))) & m
            rows[k] = rows[k] ^ t
            rows[k + j] = rows[k + j] ^ lax.shift_left(t, jnp.int32(j))
        j >>= 1
        m = (m ^ (m << j)) & 0xFFFFFFFF
    return jnp.stack(rows)


def _attn_kernel(iqT_ref, iwT_ref, qT_ref, ik_ref, k_ref, vT_ref, *rest, seq_len, topk, n_cast):
    cast_in, (o_ref, *rest) = rest[:n_cast], rest[n_cast:]
    cast_out, scratch = rest[:n_cast], rest[n_cast:]
    sc_ref, planes_ref, bias_ref, l_ref, oT_ref, tlim_ref, ties_ref = scratch
    assert K_TILE == KEY_BITS * SUBLANES
    max_chunks = seq_len // K_TILE
    i = pl.program_id(1)
    nch = i + 1
    row_k = lax.broadcasted_iota(I32, (K_TILE, Q_TILE), 0)
    lane_q = lax.broadcasted_iota(I32, (K_TILE, Q_TILE), 1)
    rel = row_k - lane_q

    def chunk_start(c):
        return pl.multiple_of(c * K_TILE, K_TILE)

    def chunk_loop(body, init):
        def pair(p, carry):
            return body(2 * p + 1, 1, body(2 * p, 0, carry))
        carry = lax.fori_loop(0, lax.shift_right_logical(nch, 1), pair, init)
        return lax.cond((nch & 1) == 1, lambda cr: body(nch - 1, 0, cr), lambda cr: cr, carry)

    def qk_logits(h, r0):
        return jnp.dot(k_ref[0, h, pl.ds(r0, K_TILE), :],
                       qT_ref[0, h * HEAD_DIM:(h + 1) * HEAD_DIM, :],
                       preferred_element_type=F32)

    def fold_rows(m):
        return jnp.sum(m.reshape(K_TILE // SUBLANES, SUBLANES, Q_TILE), axis=0)

    iw = iwT_ref[0] * ((IDX_HEADS ** -0.5) * (IDX_DIM ** -0.5))

    def score_chunk(c, slot, carry):
        for sb in range(K_TILE // SCORE_ROWS):
            r0 = pl.multiple_of(c * K_TILE + sb * SCORE_ROWS, SCORE_ROWS)
            ikc = ik_ref[0, pl.ds(r0, SCORE_ROWS), :]
            acc = jnp.zeros((SCORE_ROWS, Q_TILE), F32)
            for h in range(IDX_HEADS):
                iq_h = iqT_ref[0, h * IDX_DIM:(h + 1) * IDX_DIM, :]
                d = jnp.dot(ikc, iq_h, preferred_element_type=F32)
                acc = acc + jnp.maximum(d, 0.0) * iw[h:h + 1, :]
            acc = jnp.where(acc == 0.0, 0.0, acc)
            valid = (rel[0:SCORE_ROWS, :] + sb * SCORE_ROWS) <= (i - c) * K_TILE
            sc_ref[pl.ds(r0, SCORE_ROWS), :] = jnp.where(valid, acc, -jnp.inf)
        for h in range(EARLY_QK_HEADS):
            l_ref[h, pl.ds(chunk_start(c), K_TILE), :] = qk_logits(h, chunk_start(c))
        for lt in range(Q_TILE // LANES):
            ls = slice(lt * LANES, (lt + 1) * LANES)
            bits = pltpu.bitcast(sc_ref[pl.ds(chunk_start(c), K_TILE), ls], I32)
            u = bits ^ ((bits >> 31) | INT_MIN)
            planes_ref[c, :, :, ls] = _bit_transpose32(u.reshape(KEY_BITS, SUBLANES, LANES))
        return carry

    @pl.when(jnp.logical_and(pl.program_id(0) == 0, i == 0))
    def _():
        planes_ref[...] = jnp.zeros(planes_ref.shape, I32)

    chunk_loop(score_chunk, 0)

    def count(words):
        pops = [lax.population_count(w) for w in words]
        while len(pops) > 1:
            pops = [a + b for a, b in zip(pops[0::2], pops[1::2])]
        return jnp.sum(pops[0], axis=0, keepdims=True)

    def radix_step(r, carry):
        match, n_gt, thr_u = carry
        ones = [match[c] & planes_ref[c, r] for c in range(max_chunks)]
        n_one = count(ones)
        take = (n_gt + n_one) >= topk
        thr_u = jnp.where(take, thr_u | lax.shift_left(jnp.int32(1), KEY_BITS - 1 - r), thr_u)
        n_gt = jnp.where(take, n_gt, n_gt + n_one)
        match = tuple(jnp.where(take, ones[c], match[c] ^ ones[c]) for c in range(max_chunks))
        return match, n_gt, thr_u

    def radix_threshold():
        match0 = tuple(jnp.broadcast_to(jnp.where(c < nch, -1, 0), (SUBLANES, Q_TILE)).astype(I32)
                       for c in range(max_chunks))
        _, _, thr_u = lax.fori_loop(
            0, KEY_BITS, radix_step,
            (match0, jnp.zeros((1, Q_TILE), I32), jnp.zeros((1, Q_TILE), I32)))
        key = thr_u ^ INT_MIN
        thr = pltpu.bitcast(key ^ ((key >> 31) & 0x7FFFFFFF), F32)
        return jnp.where(thr > -jnp.inf, thr, -jnp.inf)

    def rank_counts(t):
        def body(c, slot, acc):
            sc = sc_ref[pl.ds(chunk_start(c), K_TILE), :]
            code = jnp.where(sc > t, 1 << 16, jnp.where(sc == t, 1, 0)).astype(I32)
            return acc + fold_rows(code)
        both = jnp.sum(chunk_loop(body, jnp.zeros((SUBLANES, Q_TILE), I32)),
                       axis=0, keepdims=True)
        gt = both >> 16
        return gt, gt + (both & 0xFFFF)

    def off_by(state):
        _, gt, ge, steps = state
        bad = jnp.logical_or(gt >= topk, ge < topk)
        return jnp.logical_and(jnp.max(jnp.where(bad, 1, 0)) > 0, steps < topk)

    def walk(state):
        t, gt, ge, steps = state

        lower = lambda a, b: jnp.where(a < b, a, b)
        upper = lambda a, b: jnp.where(a > b, a, b)

        def fold(op, x):
            while x.shape[0] > 1:
                half = x.shape[0] // 2
                x = op(x[:half], x[half:])
            return x

        def body(c, slot, carry):
            sc = sc_ref[pl.ds(chunk_start(c), K_TILE), :]
            return (lower(carry[0], fold(lower, jnp.where(sc > t, sc, jnp.inf))),
                    upper(carry[1], fold(upper, jnp.where(sc < t, sc, -jnp.inf))))
        up, dn = chunk_loop(body, (jnp.full((1, Q_TILE), jnp.inf, F32),
                                   jnp.full((1, Q_TILE), -jnp.inf, F32)))
        t = jnp.where(gt >= topk, up, jnp.where(ge < topk, dn, t))
        return (t,) + rank_counts(t) + (steps + 1,)

    def attend(thr, tlim, staged_heads, tile_flag):
        def logits_chunk(c, slot, m8):
            r0 = chunk_start(c)
            sc = sc_ref[pl.ds(r0, K_TILE), :]
            tie_ok = jnp.logical_and(sc == thr, (row_k + r0) <= tlim)
            sel = jnp.logical_or(sc > thr, tie_ok)
            bias_ref[slot] = jnp.where(sel, 0.0, -jnp.inf).astype(F32)
            out = []
            for h in range(N_HEADS):
                raw = l_ref[h, pl.ds(r0, K_TILE), :] if h < staged_heads else qk_logits(h, r0)
                l = raw + bias_ref[slot]
                l_ref[h, pl.ds(r0, K_TILE), :] = l
                out.append(jnp.maximum(
                    m8[h], jnp.max(l.reshape(K_TILE // SUBLANES, SUBLANES, Q_TILE), axis=0)))
            return tuple(out)

        m8 = chunk_loop(
            logits_chunk,
            tuple(jnp.full((SUBLANES, Q_TILE), -jnp.inf, F32) for _ in range(N_HEADS)))
        m_all = [jnp.max(m, axis=0, keepdims=True) for m in m8]

        oT_ref[...] = jnp.zeros(oT_ref.shape, F32)

        def pv_chunk(c, slot, s8):
            r0 = chunk_start(c)
            out = []
            for h in range(N_HEADS):
                hs = slice(h * HEAD_DIM, (h + 1) * HEAD_DIM)
                p = jnp.exp2(l_ref[h, pl.ds(r0, K_TILE), :] - m_all[h])
                out.append(s8[h] + fold_rows(p))
                oT_ref[hs, :] += jnp.dot(vT_ref[0, hs, pl.ds(r0, K_TILE)], p.astype(BF16),
                                         preferred_element_type=F32)
            return tuple(out)

        s8 = chunk_loop(
            pv_chunk, tuple(jnp.zeros((SUBLANES, Q_TILE), F32) for _ in range(N_HEADS)))
        flag = tile_flag()
        for h in range(N_HEADS):
            hs = slice(h * HEAD_DIM, (h + 1) * HEAD_DIM)
            oT_ref[hs, :] = oT_ref[hs, :] / jnp.sum(s8[h], axis=0, keepdims=True)
        o_ref[0] = oT_ref[...].T.astype(BF16)
        return flag

    thr0, gt0, ge0 = lax.cond(
        (i + 1) * Q_TILE > topk, lambda: (lambda t: (t,) + rank_counts(t))(radix_threshold()),
        lambda: (jnp.full((1, Q_TILE), -jnp.inf, F32), jnp.zeros((1, Q_TILE), I32),
                 jnp.full((1, Q_TILE), topk, I32)))
    floored = lambda t: jnp.where(t > -jnp.inf, t, jnp.finfo(F32).min)

    def limit_ties_and_attend(thr, n_gt, n_ge, staged_heads, tile_flag):
        n_eq = n_ge - n_gt
        need = topk - n_gt
        has_k = thr > -jnp.inf
        thr = floored(thr)

        tlim_ref[...] = jnp.full((SUBLANES, Q_TILE), seq_len, I32)

        surplus = jnp.logical_and(has_k, n_eq > need)
        any_surplus = jnp.max(jnp.where(surplus, 1.0, 0.0)) > 0.0
        for src, dst in zip(cast_in, cast_out):
            dst[...] = src[...].astype(BF16)

        @pl.when(any_surplus)
        def _():
            sub = lax.broadcasted_iota(I32, (SUBLANES, Q_TILE), 0)

            ties_ref[...] = jnp.zeros(ties_ref.shape, I32)

            def pack_chunk(c, slot, carry):
                eq = (sc_ref[pl.ds(chunk_start(c), K_TILE), :] == thr).reshape(
                    KEY_BITS, SUBLANES, Q_TILE)
                word = jnp.zeros((SUBLANES, Q_TILE), I32)
                for g in range(KEY_BITS):
                    word = word | jnp.where(
                        eq[g], INT_MIN if g == 0 else 1 << (KEY_BITS - 1 - g), 0)
                ties_ref[c] = word
                return carry

            chunk_loop(pack_chunk, 0)
            ties = [ties_ref[c] for c in range(max_chunks)]

            def ties_before(lim):
                tot = jnp.zeros((SUBLANES, Q_TILE), I32)
                for c in range(max_chunks):
                    g_last = (lim - (c * K_TILE + 1) - sub) >> 3
                    below = jnp.where(
                        g_last < 0, 0,
                        lax.shift_left(jnp.int32(-1),
                                       KEY_BITS - 1 - jnp.minimum(g_last, KEY_BITS - 1)))
                    tot = tot + lax.population_count(ties[c] & below)
                return jnp.sum(tot, axis=0, keepdims=True)

            def tie_step(t, lim):
                bit = lax.shift_left(jnp.int32(1), (seq_len.bit_length() - 2) - t)
                cand = lim | bit
                return jnp.where(ties_before(cand) < need, cand, lim)
            lim = lax.fori_loop(0, seq_len.bit_length() - 1, tie_step,
                                jnp.zeros((1, Q_TILE), I32))
            tlim_ref[...] = jnp.broadcast_to(jnp.where(n_eq > need, lim, seq_len),
                                             (SUBLANES, Q_TILE))

        return attend(thr, tlim_ref[0:1, :], staged_heads, tile_flag)

    off = jnp.logical_or(gt0 >= topk, ge0 < topk)
    redo = limit_ties_and_attend(thr0, gt0, ge0, EARLY_QK_HEADS,
                                 lambda: jnp.max(jnp.where(off, 1, 0)))

    @pl.when(redo > 0)
    def _():
        limit_ties_and_attend(
            *lax.while_loop(off_by, walk, (thr0, gt0, ge0, jnp.int32(0)))[:3], 0, lambda: None)


def _ffn_kernel(x_ref, attn_ref, ga_ref, mc_ref, wao_ref, wo_ref, g2_ref,
                wup_ref, fcw_ref, wdn_ref, gf_ref, out_ref,
                gbuf_ref, vbuf_ref, h1_ref, xn2_ref, merged_ref, *, tiles_per_seq):
    t = pl.program_id(0)
    tm = x_ref.shape[1]
    cur = lax.rem(t + 1, 2)
    nxt = lax.rem(t, 2)

    @pl.when(lax.rem(t - 1, tiles_per_seq) == 0)
    def _():
        zeros = jnp.zeros((SUBLANES, FFN_CHUNK), F32)
        for n in range(D_FF // FFN_CHUNK):
            gbuf_ref[n, 0:SUBLANES, :] = zeros
            vbuf_ref[n, 0:SUBLANES, :] = zeros

    def merge_next():
        ba = jnp.dot(attn_ref[0], wao_ref[...], preferred_element_type=F32)
        return (ga_ref[0] * ba + mc_ref[0]).astype(BF16)

    def norm_next(merged):
        h1n = x_ref[0] + jnp.dot(merged, wo_ref[...], preferred_element_type=F32)
        h1_ref[nxt] = h1n
        xn2_ref[nxt] = _rms(h1n, g2_ref[...]).astype(BF16)

    @pl.when(t == 0)
    def _():
        norm_next(merge_next())

    @pl.when(t > 0)
    def _():
        xn2 = xn2_ref[cur]

        def causal_conv(u, buf_ref, col0):
            buf_ref[SUBLANES:SUBLANES + tm, :] = u
            u1 = buf_ref[SUBLANES - 1:SUBLANES - 1 + tm, :]
            u2 = buf_ref[SUBLANES - 2:SUBLANES - 2 + tm, :]
            buf_ref[0:SUBLANES, :] = u[tm - SUBLANES:, :]
            w = fcw_ref[:, col0:col0 + FFN_CHUNK]
            return u2 * w[0:1, :] + u1 * w[1:2, :] + u * w[2:3, :]

        def up_proj(n):
            c0 = n * FFN_CHUNK
            ug = jnp.dot(xn2, wup_ref[:, c0:c0 + FFN_CHUNK], preferred_element_type=F32)
            uv = jnp.dot(xn2, wup_ref[:, D_FF + c0:D_FF + c0 + FFN_CHUNK],
                         preferred_element_type=F32)
            return ug, uv

        nchunk = D_FF // FFN_CHUNK
        out_ref[0] = h1_ref[cur]
        ahead = [up_proj(n) for n in range(min(FFN_LOOKAHEAD, nchunk))]
        acts = []
        for n in range(nchunk):
            c0 = n * FFN_CHUNK
            ug, uv = ahead.pop(0)
            if n + FFN_LOOKAHEAD < nchunk:
                ahead.append(up_proj(n + FFN_LOOKAHEAD))
            if n == MERGE_AT_CHUNK:
                merged_ref[...] = merge_next()
            if n == NORM_AT_CHUNK:
                norm_next(merged_ref[...])
            gt = causal_conv(ug, gbuf_ref.at[n], c0)
            val = causal_conv(uv, vbuf_ref.at[n], D_FF + c0)
            acts.append((gt * jax.nn.sigmoid(gt) * val).astype(BF16))
            if len(acts) == DOWN_GROUP or n == nchunk - 1:
                r0 = (n + 1 - len(acts)) * FFN_CHUNK
                act = acts[0] if len(acts) == 1 else jnp.concatenate(acts, axis=1)
                out_ref[0] += jnp.dot(act, wdn_ref[r0:r0 + act.shape[1], :],
                                      preferred_element_type=F32)
                acts = []

        if NORM_AT_CHUNK >= nchunk:
            norm_next(merged_ref[...])
        out_ref[0] = _rms(out_ref[0], gf_ref[...])


def _sparse_attention(iqT, iwT, qT, ik, k, vT, *, topk, cast_along=()):
    B, _, S = qT.shape
    nq = S // Q_TILE
    featT = lambda c: pl.BlockSpec((1, c, Q_TILE), lambda b, j: (b, 0, j))

    def rows_spec(w):
        rows = next(r for r in range(BF16_ROWS, w.shape[0] + 1, BF16_ROWS)
                    if w.shape[0] % r == 0 and w.shape[0] // r <= B * nq)
        last = w.shape[0] // rows - 1
        return pl.BlockSpec((rows, w.shape[1]), lambda b, j: (jnp.minimum(b * nq + j, last), 0))

    cast_specs = [rows_spec(w) for w in cast_along]
    outs = pl.pallas_call(
        functools.partial(_attn_kernel, seq_len=S, topk=topk, n_cast=len(cast_along)),
        grid=(B, nq),
        in_specs=[featT(IDX_HEADS * IDX_DIM), featT(IDX_HEADS), featT(ATTN_WIDTH),
                  pl.BlockSpec((1, S, IDX_DIM), lambda b, j: (b, 0, 0)),
                  pl.BlockSpec((1, N_HEADS, S, HEAD_DIM), lambda b, j: (b, 0, 0, 0)),
                  pl.BlockSpec((1, ATTN_WIDTH, S), lambda b, j: (b, 0, 0))] + cast_specs,
        out_specs=[pl.BlockSpec((1, Q_TILE, ATTN_WIDTH), lambda b, j: (b, j, 0))] + cast_specs,
        out_shape=[jax.ShapeDtypeStruct((B, S, ATTN_WIDTH), BF16)]
        + [jax.ShapeDtypeStruct(w.shape, BF16) for w in cast_along],
        scratch_shapes=[pltpu.VMEM((S, Q_TILE), F32),
                        pltpu.VMEM((S // K_TILE, KEY_BITS, SUBLANES, Q_TILE), I32),
                        pltpu.VMEM((2, K_TILE, Q_TILE), F32),
                        pltpu.VMEM((N_HEADS, S, Q_TILE), F32),
                        pltpu.VMEM((ATTN_WIDTH, Q_TILE), F32),
                        pltpu.VMEM((SUBLANES, Q_TILE), I32),
                        pltpu.VMEM((S // K_TILE, SUBLANES, Q_TILE), I32)],
        compiler_params=pltpu.CompilerParams(
            dimension_semantics=("arbitrary", "arbitrary"), vmem_limit_bytes=VMEM_LIMIT),
        name="attn",
    )(iqT, iwT, qT, ik, k, vT, *cast_along)
    return outs[0], tuple(outs[1:])


def _const_spec(shape):
    nd = len(shape)
    return pl.BlockSpec(shape, lambda *_: (0,) * nd, pipeline_mode=pl.Buffered(1))


def kernel(x, norm_mix_g, w_in, b_gate, w_attn_out, conv_w, w_conv_out, w_o,
           norm_ffn_g, w_up, ffn_conv_w, w_down, norm_final_g):
    B, S, D = x.shape
    assert D == D_MODEL and S % PROJ_TILE == 0 and S % FFN_TILE == 0
    assert S % Q_TILE == 0 and Q_TILE == K_TILE
    assert norm_mix_g.shape[0] == 1, "single-layer block"
    topk = min(INDEX_TOPK_MAX, S // 4)
    tm = PROJ_TILE

    w = w_in[0]
    o_q, o_k, o_v, o_iq = 0, ATTN_WIDTH, 2 * ATTN_WIDTH, 3 * ATTN_WIDTH
    o_ik = o_iq + IDX_HEADS * IDX_DIM
    o_iw = o_ik + IDX_DIM
    o_ch = o_iw + IDX_HEADS
    o_g = o_ch + 3 * CONV_WIDTH
    wqT = w[:, o_q:o_k].T.astype(BF16)
    wvT = w[:, o_v:o_iq].T.astype(BF16)
    wiqT = w[:, o_iq:o_ik].T.astype(BF16)
    wiwT = jnp.pad(w[:, o_iw:o_ch].T, ((0, BF16_ROWS - IDX_HEADS), (0, 0))).astype(BF16)
    wk = w[:, o_k:o_v].astype(BF16)
    wik = jnp.pad(w[:, o_ik:o_iw], ((0, 0), (0, LANES - IDX_DIM))).astype(BF16)
    wconv = w[:, o_ch:o_g].astype(BF16)
    wg = w[:, o_g:].astype(BF16)
    g1 = norm_mix_g.reshape(1, D)
    bg = b_gate.reshape(1, 2 * D)
    cw = conv_w[0]
    wco = w_conv_out[0].astype(BF16)

    cparams = pltpu.CompilerParams(
        dimension_semantics=("arbitrary", "arbitrary"), vmem_limit_bytes=VMEM_LIMIT)

    tok = lambda c, t: pl.BlockSpec((1, t, c), lambda b, j: (b, j, 0))
    featT = lambda c, t: pl.BlockSpec((1, c, t), lambda b, j: (b, 0, j))

    k, ik, qT, vT, iqT, iwT, ga, mc = pl.pallas_call(
        _proj_kernel,
        grid=(B, S // tm),
        in_specs=[tok(D, tm), _const_spec((1, D)),
                  _const_spec(wk.shape), _const_spec(wik.shape),
                  _const_spec(wconv.shape), _const_spec(wg.shape),
                  _const_spec(wqT.shape), _const_spec(wvT.shape),
                  _const_spec(wiqT.shape), _const_spec(wiwT.shape),
                  _const_spec(bg.shape), _const_spec(cw.shape), _const_spec(wco.shape)],
        out_specs=[pl.BlockSpec((1, N_HEADS, tm, HEAD_DIM), lambda b, j: (b, 0, j, 0)),
                   tok(IDX_DIM, tm), featT(ATTN_WIDTH, tm), featT(ATTN_WIDTH, tm),
                   featT(IDX_HEADS * IDX_DIM, tm), featT(IDX_HEADS, tm),
                   tok(D, tm), tok(D, tm)],
        out_shape=[jax.ShapeDtypeStruct((B, N_HEADS, S, HEAD_DIM), BF16),
                   jax.ShapeDtypeStruct((B, S, IDX_DIM), BF16),
                   jax.ShapeDtypeStruct((B, ATTN_WIDTH, S), BF16),
                   jax.ShapeDtypeStruct((B, ATTN_WIDTH, S), BF16),
                   jax.ShapeDtypeStruct((B, IDX_HEADS * IDX_DIM, S), BF16),
                   jax.ShapeDtypeStruct((B, IDX_HEADS, S), F32),
                   jax.ShapeDtypeStruct((B, S, D), F32),
                   jax.ShapeDtypeStruct((B, S, D), F32)],
        scratch_shapes=[pltpu.VMEM((tm + SUBLANES, CONV_WIDTH), F32)],
        compiler_params=cparams,
        name="proj",
    )(x, g1, wk, wik, wconv, wg, wqT, wvT, wiqT, wiwT, bg, cw, wco)

    attn, (wao, wo, wup, wdn) = _sparse_attention(
        iqT, iwT, qT, ik, k, vT, topk=topk,
        cast_along=(w_attn_out[0], w_o[0], w_up[0], w_down[0]))
    fcw = ffn_conv_w[0]
    g2 = norm_ffn_g.reshape(1, D)
    gf = norm_final_g.reshape(1, D)
    nchunk = D_FF // FFN_CHUNK
    assert nchunk * FFN_CHUNK == D_FF
    tf = FFN_TILE

    nt = S // tf
    n_tiles = B * nt

    def tile_in(c):
        def index(t):
            tt = jnp.minimum(t, n_tiles - 1)
            return (tt // nt, tt % nt, 0)
        return pl.BlockSpec((1, tf, c), index)

    def tile_out(t):
        tt = jnp.maximum(t - 1, 0)
        return (tt // nt, tt % nt, 0)

    out = pl.pallas_call(
        functools.partial(_ffn_kernel, tiles_per_seq=nt),
        grid=(n_tiles + 1,),
        in_specs=[tile_in(D), tile_in(ATTN_WIDTH), tile_in(D), tile_in(D),
                  _const_spec(wao.shape), _const_spec(wo.shape), _const_spec(g2.shape),
                  _const_spec(wup.shape), _const_spec(fcw.shape), _const_spec(wdn.shape),
                  _const_spec(gf.shape)],
        out_specs=pl.BlockSpec((1, tf, D), tile_out),
        out_shape=jax.ShapeDtypeStruct((B, S, D), F32),
        scratch_shapes=[pltpu.VMEM((nchunk, tf + SUBLANES, FFN_CHUNK), F32),
                        pltpu.VMEM((nchunk, tf + SUBLANES, FFN_CHUNK), F32),
                        pltpu.VMEM((2, tf, D), F32),
                        pltpu.VMEM((2, tf, D), BF16),
                        pltpu.VMEM((tf, D), BF16)],
        compiler_params=pltpu.CompilerParams(
            dimension_semantics=("arbitrary",), vmem_limit_bytes=VMEM_LIMIT),
        name="ffn",
    )(x, attn, ga, mc, wao, wo, g2, wup, fcw, wdn, gf)
    return out
```
